```python
import jax, jax.numpy as jnp
from jax import lax
import numpy as np

D_MODEL = 1024
BATCH = 8
SEQ = 2048
DEPTH = 1
DEC_BATCH = 32
DEC_SEQ = 8
PAST_LEN = 8192
PAGE_SIZE = 128

HEAD_DIM = 64
D_MIX = D_MODEL
SB_HEADS = (D_MIX // 2) // HEAD_DIM
NSA_HEADS = (D_MIX // 2) // HEAD_DIM
NSA_KV_HEADS = 2
NSA_GROUP = NSA_HEADS // NSA_KV_HEADS
CMP_BLOCK = 32
CMP_HIDDEN = 256
SEL_BLOCK = 64
N_SEL = 16
WINDOW = 512
PLE_DIM = 256
SB_Q_BLOCK = 128
NSA_Q_BLOCK = 64
EPS = 1e-6
FORCED_SCORE = 1e3
NEG_INF = -1e30
SB_W = SB_HEADS * HEAD_DIM
NSA_W = NSA_HEADS * HEAD_DIM
KV_W = NSA_KV_HEADS * HEAD_DIM
SPLITS = (SB_W, SB_W, SB_W, SB_W, NSA_W, KV_W, KV_W, KV_W, KV_W, KV_W, KV_W, 3 * NSA_HEADS, NSA_W)
N_IN = sum(SPLITS)

kernel_name = "hymba_stickbreak_nsa_decode_step"


def rms_norm(x, g):
    xf = x.astype(jnp.float32)
    y = xf * lax.rsqrt(jnp.mean(xf * xf, axis=-1, keepdims=True) + EPS)
    return (y * g.astype(jnp.float32)).astype(x.dtype)


def alibi_slopes():
    h = np.arange(1, NSA_HEADS + 1, dtype=np.float32)
    slopes = np.power(np.float32(2.0), -8.0 * h / NSA_HEADS).astype(np.float32)
    return jnp.asarray(slopes, dtype=jnp.float32).reshape(NSA_KV_HEADS, NSA_GROUP)


def masked_softmax(s, mask):
    s = jnp.where(mask, s, NEG_INF)
    m = jnp.max(s, axis=-1, keepdims=True)
    e = jnp.where(mask, jnp.exp(s - m), 0.0)
    den = jnp.sum(e, axis=-1, keepdims=True)
    return e / jnp.where(den > 0, den, 1.0)


def sweep_query_blocks(fn, qs, q_pos, block):
    T = q_pos.shape[0]
    if T <= block or T % block:
        return fn(*qs, q_pos)
    nb = T // block

    def split(a):
        return a.reshape(a.shape[0], nb, block, *a.shape[2:]).swapaxes(0, 1)

    xs = (tuple(split(a) for a in qs), q_pos.reshape(nb, block))
    out = lax.map(lambda z: fn(*z[0], z[1]), xs)
    return out.swapaxes(0, 1).reshape(out.shape[1], T, *out.shape[3:])


def gather_pages(pool, page_table):
    rows = pool[page_table]
    return rows.reshape(rows.shape[0], rows.shape[1] * rows.shape[2], *rows.shape[3:])


def stick_breaking(q, q_pos, k, v, k_pos):
    z = jnp.einsum('bqhd,bkhd->bhqk', q, k).astype(jnp.float32) * (HEAD_DIM ** -0.5)
    before = (k_pos[None, :] < q_pos[:, None])[None, None]
    log_beta = jnp.where(before, jax.nn.log_sigmoid(z), -jnp.inf)
    log_keep = jnp.where(before, jax.nn.log_sigmoid(-z), 0.0)
    log_keep_after = lax.cumsum(log_keep, axis=3, reverse=True) - log_keep
    a = jnp.exp(log_beta + log_keep_after)
    return jnp.einsum('bhqk,bkhd->bqhd', a, v.astype(jnp.float32)).astype(q.dtype)


def compress(rows, pe, w1, w2):
    B, L, G, D = rows.shape
    blocks = rows.reshape(B, L // CMP_BLOCK, CMP_BLOCK, G, D) + pe[:, None, :]
    flat = blocks.transpose(0, 1, 3, 2, 4).reshape(B, L // CMP_BLOCK, G, CMP_BLOCK * D)
    return jax.nn.silu(flat @ w1) @ w2


def nsa_attend(q, gates, q_pos, kc, vc, k_sel, v_sel, k_win, v_win, kw_start):
    f32 = jnp.float32
    B, Tq = q.shape[:2]
    G, R, D = NSA_KV_HEADS, NSA_GROUP, HEAD_DIM
    scale = D ** -0.5
    slopes = alibi_slopes()
    qg = q.reshape(B, Tq, G, R, D)
    t = q_pos[:, None]

    n_cmp = kc.shape[1]
    c_end = jnp.arange(n_cmp, dtype=jnp.int32) * CMP_BLOCK + (CMP_BLOCK - 1)
    s = jnp.einsum('bqgrd,bcgd->bqgrc', qg, kc).astype(f32) * scale
    s = s - slopes[None, None, :, :, None] * (t - c_end[None, :]).astype(f32)[None, :, None, None, :]
    p_c = masked_softmax(s, (c_end[None, :] <= t)[None, :, None, None, :])
    o_c = jnp.einsum('bqgrc,bcgd->bqgrd', p_c, vc.astype(f32))

    n_blk = k_sel.shape[1] // SEL_BLOCK
    imp = p_c.sum(axis=3).reshape(B, Tq, G, n_blk, SEL_BLOCK // CMP_BLOCK).sum(-1)
    blk = jnp.arange(n_blk, dtype=jnp.int32)[None, :]
    cur = (q_pos // SEL_BLOCK)[:, None]
    forced = ((blk == 0) | (blk == cur) | (blk == cur - 1))[None, :, None, :]
    future = (blk > cur)[None, :, None, :]
    score = jnp.where(forced, FORCED_SCORE, jnp.where(future, -1.0, imp))
    n_top = min(N_SEL, n_blk)
    _, idx = lax.top_k(score, n_top)
    idx = idx.transpose(0, 2, 1, 3)
    kb = k_sel.reshape(B, n_blk, SEL_BLOCK, G, D).transpose(0, 3, 1, 2, 4)
    vb = v_sel.reshape(B, n_blk, SEL_BLOCK, G, D).transpose(0, 3, 1, 2, 4)
    take = jax.vmap(jax.vmap(lambda a, i: a[i]))
    gk = take(kb, idx)
    gv = take(vb, idx)
    kpos = (idx[..., None] * SEL_BLOCK + jnp.arange(SEL_BLOCK, dtype=jnp.int32))[:, :, :, None]
    tq = q_pos[None, None, :, None, None, None]
    qt = qg.transpose(0, 2, 1, 3, 4)
    s = jnp.einsum('bgqrd,bgqnld->bgqrnl', qt, gk).astype(f32) * scale
    s = s - slopes[None, :, None, :, None, None] * (tq - kpos).astype(f32)
    s = s.reshape(B, G, Tq, R, n_top * SEL_BLOCK)
    p_s = masked_softmax(s, (kpos <= tq).reshape(B, G, Tq, 1, n_top * SEL_BLOCK))
    o_s = jnp.einsum('bgqrk,bgqkd->bgqrd', p_s,
                     gv.reshape(B, G, Tq, n_top * SEL_BLOCK, D).astype(f32)).transpose(0, 2, 1, 3, 4)

    n_band = WINDOW + Tq
    start = q_pos[0] - WINDOW - kw_start
    kwb = lax.dynamic_slice_in_dim(k_win, start, n_band, axis=1)
    vwb = lax.dynamic_slice_in_dim(v_win, start, n_band, axis=1)
    wpos = q_pos[0] - WINDOW + jnp.arange(n_band, dtype=jnp.int32)
    s = jnp.einsum('bqgrd,bkgd->bqgrk', qg, kwb).astype(f32) * scale
    s = s - slopes[None, None, :, :, None] * (t - wpos[None, :]).astype(f32)[None, :, None, None, :]
    wmask = (wpos[None, :] <= t) & (wpos[None, :] > t - WINDOW) & (wpos[None, :] >= 0)
    p_w = masked_softmax(s, wmask[None, :, None, None, :])
    o_w = jnp.einsum('bqgrk,bkgd->bqgrd', p_w, vwb.astype(f32))

    g = gates.reshape(B, Tq, G, R, 3).astype(f32)
    o = g[..., 0:1] * o_c + g[..., 1:2] * o_s + g[..., 2:3] * o_w
    return o.reshape(B, Tq, NSA_HEADS, D).astype(q.dtype)


def project(x, norm_g, w_in, q_g, ks_g, kw_g):
    B, T, _ = x.shape
    h = rms_norm(x, norm_g)
    z = h @ w_in
    split_at = [int(c) for c in np.cumsum(SPLITS)[:-1]]
    (q_a, k_a, v_a, gate_a, q_b, kc, vc, ks, vs, kw, vw, br, gate_b) = jnp.split(z, split_at, axis=-1)

    def heads(a, n):
        return a.reshape(B, T, n, HEAD_DIM)

    G = NSA_KV_HEADS
    sb_rows = jnp.stack([heads(k_a, SB_HEADS), heads(v_a, SB_HEADS)], axis=2)
    nsa_rows = jnp.stack([heads(kc, G), heads(vc, G), rms_norm(heads(ks, G), ks_g), heads(vs, G)], axis=2)
    win_rows = jnp.stack([rms_norm(heads(kw, G), kw_g), heads(vw, G)], axis=2)
    q_b = rms_norm(heads(q_b, NSA_HEADS), q_g)
    br = jax.nn.sigmoid(br.reshape(B, T, NSA_HEADS, 3))
    return heads(q_a, SB_HEADS), gate_a, q_b, br, gate_b, sb_rows, nsa_rows, win_rows


def decoder_layer(x, p, start_pos, past_sb, past_nsa, past_win, win_keep, lw):
    (norm_g, w_in, q_g, kc_g, ks_g, kw_g, pe_k, w1_k, w2_k, pe_v, w1_v, w2_v, w_out, w_ple, w_ple_gate) = lw
    B, T, _ = x.shape
    q_a, gate_a, q_b, br, gate_b, sb_rows, nsa_rows, win_rows = project(x, norm_g, w_in, q_g, ks_g, kw_g)
    q_pos = start_pos + jnp.arange(T, dtype=jnp.int32)

    sb = jnp.concatenate([past_sb, sb_rows], axis=1)
    sb_k, sb_v = sb[:, :, 0], sb[:, :, 1]
    k_pos = jnp.arange(sb.shape[1], dtype=jnp.int32)
    o_sb = sweep_query_blocks(lambda qb, pb: stick_breaking(qb, pb, sb_k, sb_v, k_pos), (q_a,), q_pos, SB_Q_BLOCK)

    nsa = jnp.concatenate([past_nsa, nsa_rows], axis=1)
    L = nsa.shape[1]
    l_pad = -(-L // SEL_BLOCK) * SEL_BLOCK
    nsa = jnp.pad(nsa, ((0, 0), (0, l_pad - L), (0, 0), (0, 0), (0, 0)))
    kc = rms_norm(compress(nsa[:, :, 0], pe_k, w1_k, w2_k), kc_g)
    vc = compress(nsa[:, :, 1], pe_v, w1_v, w2_v)
    k_sel, v_sel = nsa[:, :, 2], nsa[:, :, 3]
    win = jnp.concatenate([past_win, win_rows], axis=1)
    win_p = jnp.pad(win, ((0, 0), (WINDOW - past_win.shape[1], 0), (0, 0), (0, 0), (0, 0)))
    k_win, v_win = win_p[:, :, 0], win_p[:, :, 1]
    kw_start = start_pos - WINDOW
    o_nsa = sweep_query_blocks(
        lambda qb, gb, pb: nsa_attend(qb, gb, pb, kc, vc, k_sel, v_sel, k_win, v_win, kw_start),
        (q_b, br), q_pos, NSA_Q_BLOCK)

    u = jnp.concatenate([o_sb.reshape(B, T, SB_W) * jax.nn.silu(gate_a),
                         o_nsa.reshape(B, T, NSA_W) * jax.nn.silu(gate_b)], axis=-1)
    h = x + u @ w_out
    y = h + jax.nn.sigmoid(h @ w_ple_gate) * (p @ w_ple)
    return y, sb_rows, nsa_rows, win[:, win.shape[1] - win_keep:]


def setup_inputs(seed: int = 0) -> dict:
    key = jax.random.key(seed)
    ks = jax.random.split(key, 23)
    n_pages = PAST_LEN // PAGE_SIZE
    n_used = DEC_BATCH * n_pages
    n_pool = n_used + n_used // 4
    w_buf = min(WINDOW, PAST_LEN)
    f32 = jnp.float32

    def nrm(k, shape, s=1.0):
        return jax.random.normal(k, shape, f32) * s

    def gain(k, shape):
        return 1.0 + 0.01 * jax.random.normal(k, shape, f32)

    page_table = jax.random.permutation(ks[7], n_pool)[:n_used].reshape(DEC_BATCH, n_pages).astype(jnp.int32)
    return {
        "x_prompt": nrm(ks[0], (BATCH, SEQ, D_MODEL)),
        "x_sample": nrm(ks[1], (DEC_BATCH, DEC_SEQ, D_MODEL)),
        "p_prompt": nrm(ks[2], (DEPTH, BATCH, SEQ, PLE_DIM)),
        "p_sample": nrm(ks[3], (DEPTH, DEC_BATCH, DEC_SEQ, PLE_DIM)),
        "cache_sb": nrm(ks[4], (DEPTH, n_pool, PAGE_SIZE, 2, SB_HEADS, HEAD_DIM)),
        "cache_nsa": nrm(ks[5], (DEPTH, n_pool, PAGE_SIZE, 4, NSA_KV_HEADS, HEAD_DIM)),
        "state_win": nrm(ks[6], (DEPTH, DEC_BATCH, w_buf, 2, NSA_KV_HEADS, HEAD_DIM)),
        "page_table": page_table,
        "norm_g": gain(ks[8], (DEPTH, D_MODEL)),
        "w_in": nrm(ks[9], (DEPTH, D_MODEL, N_IN), D_MODEL ** -0.5),
        "q_norm_g": gain(ks[10], (DEPTH, HEAD_DIM)),
        "kcmp_norm_g": gain(ks[11], (DEPTH, HEAD_DIM)),
        "ksel_norm_g": gain(ks[12], (DEPTH, HEAD_DIM)),
        "kwin_norm_g": gain(ks[13], (DEPTH, HEAD_DIM)),
        "cmp_pos_k": nrm(ks[14], (DEPTH, CMP_BLOCK, HEAD_DIM), 0.1),
        "cmp_w1_k": nrm(ks[15], (DEPTH, CMP_BLOCK * HEAD_DIM, CMP_HIDDEN), (CMP_BLOCK * HEAD_DIM) ** -0.5),
        "cmp_w2_k": nrm(ks[16], (DEPTH, CMP_HIDDEN, HEAD_DIM), CMP_HIDDEN ** -0.5),
        "cmp_pos_v": nrm(ks[17], (DEPTH, CMP_BLOCK, HEAD_DIM), 0.1),
        "cmp_w1_v": nrm(ks[18], (DEPTH, CMP_BLOCK * HEAD_DIM, CMP_HIDDEN), (CMP_BLOCK * HEAD_DIM) ** -0.5),
        "cmp_w2_v": nrm(ks[19], (DEPTH, CMP_HIDDEN, HEAD_DIM), CMP_HIDDEN ** -0.5),
        "w_out": nrm(ks[20], (DEPTH, D_MIX, D_MODEL), D_MIX ** -0.5),
        "w_ple": nrm(ks[21], (DEPTH, PLE_DIM, D_MODEL), PLE_DIM ** -0.5),
        "w_ple_gate": nrm(ks[22], (DEPTH, D_MODEL, D_MODEL), D_MODEL ** -0.5),
    }


def reference(x_prompt, x_sample, p_prompt, p_sample, cache_sb, cache_nsa, state_win, page_table,
              norm_g, w_in, q_norm_g, kcmp_norm_g, ksel_norm_g, kwin_norm_g,
              cmp_pos_k, cmp_w1_k, cmp_w2_k, cmp_pos_v, cmp_w1_v, cmp_w2_v,
              w_out, w_ple, w_ple_gate):
    B, T = x_prompt.shape[0], x_prompt.shape[1]
    dt = x_prompt.dtype
    empty_sb = jnp.zeros((B, 0, 2, SB_HEADS, HEAD_DIM), dt)
    empty_nsa = jnp.zeros((B, 0, 4, NSA_KV_HEADS, HEAD_DIM), dt)
    empty_win = jnp.zeros((B, 0, 2, NSA_KV_HEADS, HEAD_DIM), dt)
    y_p, y_s = x_prompt, x_sample
    sb_p, sb_s, nsa_p, nsa_s, win_p, win_s = [], [], [], [], [], []
    for i in range(DEPTH):
        lw = (norm_g[i], w_in[i], q_norm_g[i], kcmp_norm_g[i], ksel_norm_g[i], kwin_norm_g[i],
              cmp_pos_k[i], cmp_w1_k[i], cmp_w2_k[i], cmp_pos_v[i], cmp_w1_v[i], cmp_w2_v[i],
              w_out[i], w_ple[i], w_ple_gate[i])
        y_p, a, b, c = decoder_layer(y_p, p_prompt[i], 0, empty_sb, empty_nsa, empty_win, min(WINDOW, T), lw)
        sb_p.append(a); nsa_p.append(b); win_p.append(c)
        y_s, a, b, c = decoder_layer(y_s, p_sample[i], PAST_LEN,
                                     gather_pages(cache_sb[i], page_table),
                                     gather_pages(cache_nsa[i], page_table),
                                     state_win[i], state_win.shape[2], lw)
        sb_s.append(a); nsa_s.append(b); win_s.append(c)
    return (y_p, y_s, jnp.stack(sb_p), jnp.stack(sb_s), jnp.stack(nsa_p), jnp.stack(nsa_s), jnp.stack(win_p), jnp.stack(win_s))
```

```python
import functools

import numpy as np
import jax
import jax.numpy as jnp
from jax import lax
from jax.experimental import pallas as pl
from jax.experimental.pallas import tpu as pltpu

F32 = jnp.float32
BF16 = jnp.bfloat16

HEAD_DIM = 64
SB_HEADS = 8
NSA_HEADS = 8
NSA_KV_HEADS = 2
NSA_GROUP = NSA_HEADS // NSA_KV_HEADS
CMP_BLOCK = 32
CMP_HIDDEN = 256
SEL_BLOCK = 64
N_SEL = 16
WINDOW = 512
PAGE_SIZE = 128
EPS = 1e-6
FORCED_SCORE = 1e3
NEG = -1e30
LOWEST = -3e38
SCALE = HEAD_DIM ** -0.5
SB_W = SB_HEADS * HEAD_DIM
NSA_W = NSA_HEADS * HEAD_DIM
KV_W = NSA_KV_HEADS * HEAD_DIM
LANES = 128
SLOPES = [float(np.power(np.float32(2.0), np.float32(-8.0 * h / NSA_HEADS)))
          for h in range(1, NSA_HEADS + 1)]

C_QA, C_SBK, C_GA, C_QB, C_KC, C_KS, C_VS, C_KW, C_VW, C_BR = (
    0, SB_W, 3 * SB_W, 4 * SB_W, 4 * SB_W + NSA_W, 4 * SB_W + NSA_W + 2 * KV_W,
    4 * SB_W + NSA_W + 3 * KV_W, 4 * SB_W + NSA_W + 4 * KV_W, 4 * SB_W + NSA_W + 5 * KV_W,
    4 * SB_W + NSA_W + 6 * KV_W)
C_GB = C_BR + 3 * NSA_HEADS

VMEM_LIMIT = 48 * 1024 * 1024


def _dot(a, b):
    return jnp.dot(a, b, preferred_element_type=F32)


def _dot_nt(a, b):
    return lax.dot_general(a, b, (((1,), (1,)), ((), ())), preferred_element_type=F32)


def _iota(shape, dim):
    return lax.broadcasted_iota(jnp.int32, shape, dim)


def _div(x, n):
    assert n & (n - 1) == 0
    return lax.shift_right_logical(x, jnp.int32(n.bit_length() - 1))


def _mod(x, n):
    assert n & (n - 1) == 0
    return lax.bitwise_and(x, jnp.int32(n - 1))


def _pair_rms(x, g2):
    lo = _iota(x.shape, 1) < HEAD_DIM
    x2 = x * x
    s0 = jnp.sum(jnp.where(lo, x2, 0.0), axis=-1, keepdims=True)
    s1 = jnp.sum(jnp.where(lo, 0.0, x2), axis=-1, keepdims=True)
    r = jnp.where(lo, lax.rsqrt(s0 * (1.0 / HEAD_DIM) + EPS), lax.rsqrt(s1 * (1.0 / HEAD_DIM) + EPS))
    return x * r * g2


def _proj_kernel(x_ref, ng_ref, wm_ref, wbr_ref, wgb_ref, qg_ref, ksg_ref, kwg_ref,
                 qa_ref, sb_ref, gates_ref, qb_ref, nsa_ref, win_ref, br_ref):
    x = x_ref[...]
    ms = jnp.mean(x * x, axis=-1, keepdims=True)
    h = (x * lax.rsqrt(ms + EPS) * ng_ref[...]).astype(BF16)

    def z(lo, hi):
        return _dot(h, wm_ref[:, lo:hi])

    qa_ref[...] = z(C_QA, C_SBK) * SCALE
    sb_ref[...] = z(C_SBK, C_GA)
    ga = z(C_GA, C_QB)
    gates_ref[:, 0:SB_W] = ga * jax.nn.sigmoid(ga)
    gb = _dot(h, wgb_ref[...])
    gates_ref[:, SB_W:SB_W + NSA_W] = gb * jax.nn.sigmoid(gb)
    qb = z(C_QB, C_KC)
    for pr in range(NSA_W // LANES):
        sl = slice(pr * LANES, (pr + 1) * LANES)
        qb_ref[:, sl] = _pair_rms(qb[:, sl], qg_ref[...]) * SCALE
    nsa_ref[:, 0:2 * KV_W] = z(C_KC, C_KS)
    nsa_ref[:, 2 * KV_W:3 * KV_W] = _pair_rms(z(C_KS, C_VS), ksg_ref[...])
    nsa_ref[:, 3 * KV_W:4 * KV_W] = z(C_VS, C_KW)
    win_ref[:, 0:KV_W] = _pair_rms(z(C_KW, C_VW), kwg_ref[...])
    win_ref[:, KV_W:2 * KV_W] = z(C_VW, C_BR)
    br_ref[...] = jax.nn.sigmoid(_dot(h, wbr_ref[...]))


def _project(x2d, lw, tm):
    n, d = x2d.shape
    full = lambda a: pl.BlockSpec(a.shape, lambda i: (0,) * a.ndim)
    row = lambda w: pl.BlockSpec((tm, w), lambda i: (i, 0))
    ins = [x2d, lw["norm_g"], lw["w_main"], lw["w_br"], lw["w_gb"], lw["q_g2"], lw["ks_g2"], lw["kw_g2"]]
    widths = [SB_W, 2 * SB_W, SB_W + NSA_W, NSA_W, 4 * KV_W, 2 * KV_W, LANES]
    return pl.pallas_call(
        _proj_kernel,
        grid=(n // tm,),
        in_specs=[row(d)] + [full(a) for a in ins[1:]],
        out_specs=[row(w) for w in widths],
        out_shape=[jax.ShapeDtypeStruct((n, w), F32) for w in widths],
        compiler_params=pltpu.CompilerParams(dimension_semantics=("arbitrary",),
                                             vmem_limit_bytes=VMEM_LIMIT),
        name="proj",
    )(*ins)


def _out_kernel(osb_ref, onsa_ref, gates_ref, x_ref, p_ref, wo_ref, wpg_ref, wp_ref, y_ref):
    ua = (osb_ref[...] * gates_ref[:, 0:SB_W]).astype(BF16)
    ub = (onsa_ref[...] * gates_ref[:, SB_W:SB_W + NSA_W]).astype(BF16)
    h = x_ref[...] + (_dot(ua, wo_ref[0:SB_W, :]) + _dot(ub, wo_ref[SB_W:SB_W + NSA_W, :]))
    gate = jax.nn.sigmoid(_dot(h.astype(BF16), wpg_ref[...]))
    y_ref[...] = h + gate * _dot(p_ref[...].astype(BF16), wp_ref[...])


def _output(osb, onsa, gates, x2d, p2d, lw, tm):
    n, d = x2d.shape
    full = lambda a: pl.BlockSpec(a.shape, lambda i: (0,) * a.ndim)
    row = lambda w: pl.BlockSpec((tm, w), lambda i: (i, 0))
    ins = [osb, onsa, gates, x2d, p2d, lw["w_out"], lw["w_pg"], lw["w_ple"]]
    return pl.pallas_call(
        _out_kernel,
        grid=(n // tm,),
        in_specs=[row(a.shape[1]) for a in ins[:5]] + [full(a) for a in ins[5:]],
        out_specs=row(d),
        out_shape=jax.ShapeDtypeStruct((n, d), F32),
        compiler_params=pltpu.CompilerParams(dimension_semantics=("arbitrary",),
                                             vmem_limit_bytes=VMEM_LIMIT),
        name="outproj",
    )(*ins)


def _sb_chunk(z, v_bf, r_col, u_bf, before):
    sp = jnp.log1p(jnp.exp(-jnp.abs(z)))
    ls = jnp.minimum(z, 0.0) - sp
    lk = jnp.minimum(-z, 0.0) - sp
    if before is not None:
        ls = jnp.where(before, ls, NEG)
        lk = jnp.where(before, lk, 0.0)
    hi = lk.astype(BF16)
    lo = (lk - hi.astype(F32)).astype(BF16)
    cs = _dot(hi, u_bf) + _dot(lo, u_bf)
    a = jnp.exp(ls + cs + r_col)
    contrib = _dot(a.astype(BF16), v_bf)
    return contrib, r_col + jnp.sum(lk, axis=-1, keepdims=True)


def _suffix_matrix(n):
    j = np.arange(n)
    return jnp.asarray((j[:, None] > j[None, :]).astype(np.float32), dtype=BF16)


def _sb_prompt_kernel(q_ref, k_ref, v_ref, u_ref, o_ref, *, tq):
    i = pl.program_id(2)
    q = q_ref[...]
    lane = _iota(q.shape, 1)
    before = _iota((tq, tq), 1) < _iota((tq, tq), 0)
    u = u_ref[...]
    outs = []
    for hh in range(2):
        keep = (lane < HEAD_DIM) if hh == 0 else (lane >= HEAD_DIM)
        qm = jnp.where(keep, q, 0.0).astype(BF16)
        d0 = pl.multiple_of(i * tq, tq)
        kd = k_ref[pl.ds(d0, tq), :].astype(BF16)
        vd = v_ref[pl.ds(d0, tq), :].astype(BF16)
        acc, r = _sb_chunk(_dot_nt(qm, kd), vd, jnp.zeros((tq, 1), F32), u, before)

        def body(j, carry, qm=qm):
            acc, r = carry
            c0 = pl.multiple_of((i - 1 - j) * tq, tq)
            kc = k_ref[pl.ds(c0, tq), :].astype(BF16)
            vc = v_ref[pl.ds(c0, tq), :].astype(BF16)
            contrib, r2 = _sb_chunk(_dot_nt(qm, kc), vc, r, u, None)
            return acc + contrib, r2

        acc, r = lax.fori_loop(0, i, body, (acc, r))
        outs.append(acc)
    o_ref[...] = jnp.where(lane < HEAD_DIM, outs[0], outs[1])


def _sb_prompt(qa, sb_rows, b, t, tq):
    nq = t // tq
    npair = SB_W // LANES
    return pl.pallas_call(
        functools.partial(_sb_prompt_kernel, tq=tq),
        grid=(b, npair, nq),
        in_specs=[
            pl.BlockSpec((tq, LANES), lambda bi, hp, i: (bi * nq + i, hp)),
            pl.BlockSpec((t, LANES), lambda bi, hp, i: (bi, hp)),
            pl.BlockSpec((t, LANES), lambda bi, hp, i: (bi, npair + hp)),
            pl.BlockSpec((tq, tq), lambda bi, hp, i: (0, 0)),
        ],
        out_specs=pl.BlockSpec((tq, LANES), lambda bi, hp, i: (bi * nq + i, hp)),
        out_shape=jax.ShapeDtypeStruct((b * t, SB_W), F32),
        compiler_params=pltpu.CompilerParams(
            dimension_semantics=("arbitrary", "arbitrary", "arbitrary"),
            vmem_limit_bytes=VMEM_LIMIT),
        name="sb_prompt",
    )(qa, sb_rows, sb_rows, _suffix_matrix(tq))


def _sb_sample_kernel(pt_ref, q_ref, new_ref, *rest, n_pp, n_steps, tq):
    del pt_ref
    page_refs = rest[:n_pp]
    u_ref, o_ref, acc_ref, r_ref = rest[n_pp:]
    s = pl.program_id(1)
    rows = SB_HEADS * tq
    q = q_ref[...]
    bd = _div(_iota((rows, SB_W), 0), tq) == _div(_iota((rows, SB_W), 1), HEAD_DIM)
    qbd = jnp.where(bd, jnp.concatenate([q] * SB_HEADS, axis=0), 0.0).astype(BF16)
    u = u_ref[...]

    def do_page(kv, before):
        k = kv[:, 0:SB_W].astype(BF16)
        v = kv[:, SB_W:2 * SB_W].astype(BF16)
        contrib, r2 = _sb_chunk(_dot_nt(qbd, k), v, r_ref[...], u, before)
        acc_ref[...] = acc_ref[...] + contrib
        r_ref[...] = r2

    @pl.when(s == 0)
    def _():
        acc_ref[...] = jnp.zeros_like(acc_ref)
        r_ref[...] = jnp.zeros_like(r_ref)
        kv = jnp.concatenate([new_ref[...], jnp.zeros((PAGE_SIZE - tq, 2 * SB_W), F32)], axis=0)
        t_loc = _mod(_iota((rows, PAGE_SIZE), 0), tq)
        do_page(kv, _iota((rows, PAGE_SIZE), 1) < t_loc)

    @pl.when(s > 0)
    def _():
        for j in range(n_pp):
            do_page(page_refs[j][...], None)

    @pl.when(s == n_steps - 1)
    def _():
        accm = jnp.where(bd, acc_ref[...], 0.0)
        o = accm[0:tq]
        for h in range(1, SB_HEADS):
            o = o + accm[h * tq:(h + 1) * tq]
        o_ref[...] = o


def _sb_sample(qa, sb_new, cache_sb, page_table, db, tq, n_pp):
    n_pages = page_table.shape[1]
    n_steps = n_pages // n_pp + 1
    pool = cache_sb.shape[0]
    cache2 = cache_sb.reshape(pool, PAGE_SIZE, 2 * SB_W)

    def page_spec(j):
        def imap(bi, s, pt):
            lp = n_pages - 1 - (jnp.maximum(s - 1, 0) * n_pp + j)
            return (pt[bi, lp], 0, 0)
        return pl.BlockSpec((None, PAGE_SIZE, 2 * SB_W), imap)

    grid_spec = pltpu.PrefetchScalarGridSpec(
        num_scalar_prefetch=1,
        grid=(db, n_steps),
        in_specs=[pl.BlockSpec((tq, SB_W), lambda bi, s, pt: (bi, 0)),
                  pl.BlockSpec((tq, 2 * SB_W), lambda bi, s, pt: (bi, 0))]
                 + [page_spec(j) for j in range(n_pp)]
                 + [pl.BlockSpec((PAGE_SIZE, PAGE_SIZE), lambda bi, s, pt: (0, 0))],
        out_specs=pl.BlockSpec((tq, SB_W), lambda bi, s, pt: (bi, 0)),
        scratch_shapes=[pltpu.VMEM((SB_HEADS * tq, SB_W), F32),
                        pltpu.VMEM((SB_HEADS * tq, 1), F32)],
    )
    return pl.pallas_call(
        functools.partial(_sb_sample_kernel, n_pp=n_pp, n_steps=n_steps, tq=tq),
        grid_spec=grid_spec,
        out_shape=jax.ShapeDtypeStruct((db * tq, SB_W), F32),
        compiler_params=pltpu.CompilerParams(dimension_semantics=("arbitrary", "arbitrary"),
                                             vmem_limit_bytes=VMEM_LIMIT),
        name="sb_sample",
    )(page_table, qa, sb_new, *([cache2] * n_pp), _suffix_matrix(PAGE_SIZE))


def _compress(xk_ref, xv_ref, m, pek_ref, pev_ref, w1k_ref, w1v_ref, w2k_ref, w2v_ref, kcg_ref):
    hk = jnp.zeros((m, NSA_KV_HEADS * CMP_HIDDEN), F32)
    hv = jnp.zeros((m, NSA_KV_HEADS * CMP_HIDDEN), F32)
    for p in range(CMP_BLOCK):
        xk = (xk_ref[pl.ds(p, m, stride=CMP_BLOCK), :] + pek_ref[p:p + 1, :]).astype(BF16)
        xv = (xv_ref[pl.ds(p, m, stride=CMP_BLOCK), :] + pev_ref[p:p + 1, :]).astype(BF16)
        hk = hk + _dot(xk, w1k_ref[p])
        hv = hv + _dot(xv, w1v_ref[p])
    kc = _dot((hk * jax.nn.sigmoid(hk)).astype(BF16), w2k_ref[...])
    vc = _dot((hv * jax.nn.sigmoid(hv)).astype(BF16), w2v_ref[...])
    return _pair_rms(kc, kcg_ref[...]), vc


def _cmp_prompt_kernel(xk_ref, xv_ref, pek_ref, pev_ref, w1k_ref, w1v_ref, w2k_ref, w2v_ref, kcg_ref,
                       kc_ref, vc_ref, *, m):
    kc, vc = _compress(xk_ref, xv_ref, m, pek_ref, pev_ref, w1k_ref, w1v_ref, w2k_ref, w2v_ref,
                       kcg_ref)
    pad = jnp.zeros((kc_ref.shape[0] - m, KV_W), F32)
    kc_ref[...] = jnp.concatenate([kc, pad], axis=0)
    vc_ref[...] = jnp.concatenate([vc, pad], axis=0)


def _cmp_weights(lw):
    return [lw["pe_k2"], lw["pe_v2"], lw["w1k_bd"], lw["w1v_bd"], lw["w2k_bd"], lw["w2v_bd"], lw["kc_g2"]]


def _cmp_prompt(nsa_rows, lw, b, t, nbp):
    m = t // CMP_BLOCK
    ws = _cmp_weights(lw)
    full = lambda a: pl.BlockSpec(a.shape, lambda bi: (0,) * a.ndim)
    return pl.pallas_call(
        functools.partial(_cmp_prompt_kernel, m=m),
        grid=(b,),
        in_specs=[pl.BlockSpec((t, KV_W), lambda bi: (bi, 0)),
                  pl.BlockSpec((t, KV_W), lambda bi: (bi, 1))] + [full(a) for a in ws],
        out_specs=[pl.BlockSpec((None, 2 * nbp, KV_W), lambda bi: (bi, 0, 0))] * 2,
        out_shape=[jax.ShapeDtypeStruct((b, 2 * nbp, KV_W), F32)] * 2,
        compiler_params=pltpu.CompilerParams(dimension_semantics=("arbitrary",),
                                             vmem_limit_bytes=VMEM_LIMIT),
        name="cmp_prompt",
    )(nsa_rows, nsa_rows, *ws)


def _topk_mask(score, n_top):
    lanef = _iota(score.shape, 1).astype(F32)
    sel = jnp.zeros(score.shape, F32)
    s = score
    for _ in range(n_top):
        m = jnp.max(s, axis=-1, keepdims=True)
        idx = jnp.min(jnp.where(s == m, lanef, 1e9), axis=-1, keepdims=True)
        pick = lanef == idx
        sel = jnp.where(pick, 1.0, sel)
        s = jnp.where(pick, LOWEST, s)
    return sel


def _softmax_parts(s_list, m_list):
    mx = None
    for s in s_list:
        c = jnp.max(s, axis=-1, keepdims=True)
        mx = c if mx is None else jnp.maximum(mx, c)
    es = [jnp.where(mk, jnp.exp(s - mx), 0.0) for s, mk in zip(s_list, m_list)]
    den = None
    for e in es:
        c = jnp.sum(e, axis=-1, keepdims=True)
        den = c if den is None else den + c
    inv = 1.0 / jnp.where(den > 0, den, 1.0)
    return [e * inv for e in es]


def _nsa_attend(q_ref, br_ref, kc_ref, vc_ref, sel_ref, sel_cols, win_ref, o_ref, *,
                t0, tq, nbp, n_blk, n_sel_chunks, sel_kc, win_row0, win_pos0, win_len):
    rg = NSA_GROUP
    rows = rg * tq
    tcol = t0 + _mod(_iota((rows, 1), 0), tq)
    tq_col = t0 + _iota((tq, 1), 0)
    lane_q = _iota((tq, LANES), 1)
    br = br_ref[...]
    for g in range(NSA_KV_HEADS):
        keep = (lane_q < HEAD_DIM) if g == 0 else (lane_q >= HEAD_DIM)
        parts = []
        for r in range(rg):
            h = g * rg + r
            qp = q_ref[:, (h // 2) * LANES:(h // 2 + 1) * LANES]
            if (h % 2) != g:
                qp = pltpu.roll(qp, HEAD_DIM, 1)
            parts.append(jnp.where(keep, qp, 0.0))
        qs = jnp.concatenate(parts, axis=0).astype(BF16)
        slope = jnp.concatenate([jnp.full((tq, 1), SLOPES[g * rg + r], F32) for r in range(rg)], axis=0)

        ci = _iota((1, nbp), 1)
        s_list, m_list, v_list = [], [], []
        for par in range(2):
            kx = kc_ref[pl.ds(par, nbp, stride=2), :].astype(BF16)
            v_list.append(vc_ref[pl.ds(par, nbp, stride=2), :].astype(BF16))
            c_end = ci * (2 * CMP_BLOCK) + (par * CMP_BLOCK + CMP_BLOCK - 1)
            s = _dot_nt(qs, kx) - slope * (tcol - c_end).astype(F32)
            mk = c_end <= tcol
            s_list.append(jnp.where(mk, s, NEG))
            m_list.append(mk)
        p_list = _softmax_parts(s_list, m_list)
        o_c = _dot(p_list[0].astype(BF16), v_list[0]) + _dot(p_list[1].astype(BF16), v_list[1])
        pp = p_list[0] + p_list[1]
        imp = pp[0:tq]
        for r in range(1, rg):
            imp = imp + pp[r * tq:(r + 1) * tq]

        blk = _iota((tq, nbp), 1)
        cur = _div(tq_col, SEL_BLOCK)
        score = jnp.where(blk > cur, -1.0, imp)
        score = jnp.where(blk == 0, FORCED_SCORE, score)
        score = jnp.where(blk == cur, FORCED_SCORE, score)
        score = jnp.where(blk == cur - 1, FORCED_SCORE, score)
        score = jnp.where(blk < n_blk, score, LOWEST)
        sel = _topk_mask(score, min(N_SEL, n_blk))
        selb = jnp.concatenate([sel] * rg, axis=0).astype(BF16)

        def sel_body(j, carry, qs=qs, slope=slope, selb=selb):
            m, l, acc = carry
            start = pl.multiple_of(j * sel_kc, sel_kc)
            k = sel_ref[pl.ds(start, sel_kc), sel_cols[0]:sel_cols[0] + KV_W].astype(BF16)
            v = sel_ref[pl.ds(start, sel_kc), sel_cols[1]:sel_cols[1] + KV_W].astype(BF16)
            kpos = start + _iota((1, sel_kc), 1)
            s = _dot_nt(qs, k) - slope * (tcol - kpos).astype(F32)
            kb = _div(start + _iota((sel_kc, nbp), 0), SEL_BLOCK)
            et = jnp.where(kb == _iota((sel_kc, nbp), 1), 1.0, 0.0).astype(BF16)
            selk = _dot_nt(selb, et)
            mk = jnp.where(kpos <= tcol, selk, 0.0) > 0.5
            s = jnp.where(mk, s, NEG)
            m_new = jnp.maximum(m, jnp.max(s, axis=-1, keepdims=True))
            alpha = jnp.exp(m - m_new)
            p = jnp.where(mk, jnp.exp(s - m_new), 0.0)
            l = alpha * l + jnp.sum(p, axis=-1, keepdims=True)
            acc = alpha * acc + _dot(p.astype(BF16), v)
            return m_new, l, acc

        init = (jnp.full((rows, 1), NEG, F32), jnp.zeros((rows, 1), F32), jnp.zeros((rows, KV_W), F32))
        _, l_s, acc_s = lax.fori_loop(0, n_sel_chunks, sel_body, init)
        o_s = acc_s * (1.0 / jnp.where(l_s > 0, l_s, 1.0))

        kw = win_ref[pl.ds(win_row0, win_len), 0:KV_W].astype(BF16)
        vw = win_ref[pl.ds(win_row0, win_len), KV_W:2 * KV_W].astype(BF16)
        wpos = win_pos0 + _iota((1, win_len), 1)
        s = _dot_nt(qs, kw) - slope * (tcol - wpos).astype(F32)
        mk = jnp.where(wpos <= tcol, jnp.where(wpos > tcol - WINDOW, 1.0, 0.0), 0.0) > 0.5
        (p_w,) = _softmax_parts([jnp.where(mk, s, NEG)], [mk])
        o_w = _dot(p_w.astype(BF16), vw)

        def gate_col(j):
            return jnp.concatenate(
                [br[:, (g * rg + r) * 3 + j:(g * rg + r) * 3 + j + 1] for r in range(rg)], axis=0)

        o = gate_col(0) * o_c + gate_col(1) * o_s + gate_col(2) * o_w
        pieces = []
        for r in range(rg):
            piece = o[r * tq:(r + 1) * tq]
            if (r % 2) != g:
                piece = pltpu.roll(piece, HEAD_DIM, 1)
            pieces.append(piece)
        for pr in range(rg // 2):
            slot = g * (rg // 2) + pr
            o_ref[:, slot * LANES:(slot + 1) * LANES] = jnp.where(
                lane_q < HEAD_DIM, pieces[2 * pr], pieces[2 * pr + 1])


def _nsa_prompt_kernel(q_ref, br_ref, kc_ref, vc_ref, sel_ref, win_ref, o_ref, *,
                       tq, nbp, n_blk, sel_kc, win_len):
    t0 = pl.program_id(1) * tq
    n_chunks = (t0 + tq + sel_kc - 1) // sel_kc
    w0 = pl.multiple_of(jnp.maximum(t0 - WINDOW, 0), tq)
    _nsa_attend(q_ref, br_ref, kc_ref, vc_ref, sel_ref, (0, KV_W), win_ref, o_ref,
                t0=t0, tq=tq, nbp=nbp, n_blk=n_blk, n_sel_chunks=n_chunks, sel_kc=sel_kc,
                win_row0=w0, win_pos0=w0, win_len=win_len)


def _nsa_prompt(qb, br, kc, vc, nsa_rows, win_rows, b, t, tq, nbp, sel_kc):
    nq = t // tq
    n_blk = t // SEL_BLOCK
    return pl.pallas_call(
        functools.partial(_nsa_prompt_kernel, tq=tq, nbp=nbp, n_blk=n_blk, sel_kc=sel_kc,
                          win_len=WINDOW + tq),
        grid=(b, nq),
        in_specs=[
            pl.BlockSpec((tq, NSA_W), lambda bi, i: (bi * nq + i, 0)),
            pl.BlockSpec((tq, LANES), lambda bi, i: (bi * nq + i, 0)),
            pl.BlockSpec((None, 2 * nbp, KV_W), lambda bi, i: (bi, 0, 0)),
            pl.BlockSpec((None, 2 * nbp, KV_W), lambda bi, i: (bi, 0, 0)),
            pl.BlockSpec((t, 2 * KV_W), lambda bi, i: (bi, 1)),
            pl.BlockSpec((t, 2 * KV_W), lambda bi, i: (bi, 0)),
        ],
        out_specs=pl.BlockSpec((tq, NSA_W), lambda bi, i: (bi * nq + i, 0)),
        out_shape=jax.ShapeDtypeStruct((b * t, NSA_W), F32),
        compiler_params=pltpu.CompilerParams(dimension_semantics=("arbitrary", "arbitrary"),
                                             vmem_limit_bytes=VMEM_LIMIT),
        name="nsa_prompt",
    )(qb, br, kc, vc, nsa_rows, win_rows)


def _nsa_sample_kernel(pt_ref, q_ref, br_ref, new_ref, wnew_ref, wpast_ref,
                       pek_ref, pev_ref, w1k_ref, w1v_ref, w2k_ref, w2v_ref, kcg_ref, *rest,
                       n_pp, n_steps, tq, past, m_c, nbp, n_blk, sel_kc, n_sel_chunks, win_len):
    del pt_ref
    page_refs = rest[:n_pp]
    o_ref, bufk, bufv, buf, kcs, vcs, wbuf = rest[n_pp:]
    s = pl.program_id(1)
    for j in range(n_pp):
        r0 = pl.multiple_of((s * n_pp + j) * PAGE_SIZE, PAGE_SIZE)
        bufk[pl.ds(r0, PAGE_SIZE), :] = page_refs[j][:, 0:KV_W]
        bufv[pl.ds(r0, PAGE_SIZE), :] = page_refs[j][:, KV_W:2 * KV_W]
        buf[pl.ds(r0, PAGE_SIZE), :] = page_refs[j][:, 2 * KV_W:4 * KV_W]

    @pl.when(s == n_steps - 1)
    def _():
        n_tail = buf.shape[0] - past - tq
        bufk[past:past + tq, :] = new_ref[:, 0:KV_W]
        bufv[past:past + tq, :] = new_ref[:, KV_W:2 * KV_W]
        buf[past:past + tq, :] = new_ref[:, 2 * KV_W:4 * KV_W]
        bufk[past + tq:, :] = jnp.zeros((n_tail, KV_W), F32)
        bufv[past + tq:, :] = jnp.zeros((n_tail, KV_W), F32)
        buf[past + tq:, :] = jnp.zeros((n_tail, 2 * KV_W), F32)
        kc, vc = _compress(bufk, bufv, m_c, pek_ref, pev_ref, w1k_ref, w1v_ref, w2k_ref, w2v_ref,
                           kcg_ref)
        pad = jnp.zeros((kcs.shape[0] - m_c, KV_W), F32)
        kcs[...] = jnp.concatenate([kc, pad], axis=0)
        vcs[...] = jnp.concatenate([vc, pad], axis=0)
        wbuf[0:WINDOW, :] = wpast_ref[...]
        wbuf[WINDOW:WINDOW + tq, :] = wnew_ref[...]
        wbuf[WINDOW + tq:, :] = jnp.zeros((win_len - WINDOW - tq, 2 * KV_W), F32)
        _nsa_attend(q_ref, br_ref, kcs, vcs, buf, (0, KV_W), wbuf, o_ref,
                    t0=past, tq=tq, nbp=nbp, n_blk=n_blk, n_sel_chunks=n_sel_chunks, sel_kc=sel_kc,
                    win_row0=0, win_pos0=past - WINDOW, win_len=win_len)


def _nsa_sample(qb, br, nsa_new, win_new, state_win, cache_nsa, page_table, lw, db, tq, n_pp, sel_kc):
    n_pages = page_table.shape[1]
    n_steps = n_pages // n_pp
    past = n_pages * PAGE_SIZE
    l_pad = -(-(past + tq) // SEL_BLOCK) * SEL_BLOCK
    n_blk = l_pad // SEL_BLOCK
    n_cmp = l_pad // CMP_BLOCK
    m_c = -(-n_cmp // 8) * 8
    nbp = -(-n_blk // LANES) * LANES
    n_sel_chunks = -(-l_pad // sel_kc)
    buf_rows = max(CMP_BLOCK * m_c, n_sel_chunks * sel_kc)
    win_len = WINDOW + LANES
    pool = cache_nsa.shape[0]
    cache2 = cache_nsa.reshape(pool, PAGE_SIZE, 4 * KV_W)
    wpast = state_win.reshape(db, WINDOW, 2 * KV_W)
    ws = _cmp_weights(lw)

    def page_spec(j):
        return pl.BlockSpec((None, PAGE_SIZE, 4 * KV_W), lambda bi, s, pt: (pt[bi, s * n_pp + j], 0, 0))

    full = lambda a: pl.BlockSpec(a.shape, lambda bi, s, pt: (0,) * a.ndim)
    grid_spec = pltpu.PrefetchScalarGridSpec(
        num_scalar_prefetch=1,
        grid=(db, n_steps),
        in_specs=[pl.BlockSpec((tq, NSA_W), lambda bi, s, pt: (bi, 0)),
                  pl.BlockSpec((tq, LANES), lambda bi, s, pt: (bi, 0)),
                  pl.BlockSpec((tq, 4 * KV_W), lambda bi, s, pt: (bi, 0)),
                  pl.BlockSpec((tq, 2 * KV_W), lambda bi, s, pt: (bi, 0)),
                  pl.BlockSpec((None, WINDOW, 2 * KV_W), lambda bi, s, pt: (bi, 0, 0))]
                 + [full(a) for a in ws]
                 + [page_spec(j) for j in range(n_pp)],
        out_specs=pl.BlockSpec((tq, NSA_W), lambda bi, s, pt: (bi, 0)),
        scratch_shapes=[pltpu.VMEM((buf_rows, KV_W), F32),
                        pltpu.VMEM((buf_rows, KV_W), F32),
                        pltpu.VMEM((buf_rows, 2 * KV_W), F32),
                        pltpu.VMEM((2 * nbp, KV_W), F32),
                        pltpu.VMEM((2 * nbp, KV_W), F32),
                        pltpu.VMEM((win_len, 2 * KV_W), F32)],
    )
    return pl.pallas_call(
        functools.partial(_nsa_sample_kernel, n_pp=n_pp, n_steps=n_steps, tq=tq, past=past, m_c=m_c,
                          nbp=nbp, n_blk=n_blk, sel_kc=sel_kc, n_sel_chunks=n_sel_chunks,
                          win_len=win_len),
        grid_spec=grid_spec,
        out_shape=jax.ShapeDtypeStruct((db * tq, NSA_W), F32),
        compiler_params=pltpu.CompilerParams(dimension_semantics=("arbitrary", "arbitrary"),
                                             vmem_limit_bytes=VMEM_LIMIT),
        name="nsa_sample",
    )(page_table, qb, br, nsa_new, win_new, wpast, *ws, *([cache2] * n_pp))


def _block_diag2(a):
    z = jnp.zeros_like(a)
    return jnp.concatenate([jnp.concatenate([a, z], axis=-1), jnp.concatenate([z, a], axis=-1)], axis=-2)


def _layer_weights(norm_g, w_in, q_g, kc_g, ks_g, kw_g, pe_k, w1_k, w2_k, pe_v, w1_v, w2_v,
                   w_out, w_ple, w_ple_gate):
    two = lambda g: jnp.concatenate([g, g]).reshape(1, 2 * HEAD_DIM)
    w_br = jnp.pad(w_in[:, C_BR:C_GB], ((0, 0), (0, LANES - 3 * NSA_HEADS)))
    return {
        "norm_g": norm_g.reshape(1, -1),
        "w_main": w_in[:, :C_BR].astype(BF16),
        "w_br": w_br.astype(BF16),
        "w_gb": w_in[:, C_GB:].astype(BF16),
        "q_g2": two(q_g), "ks_g2": two(ks_g), "kw_g2": two(kw_g), "kc_g2": two(kc_g),
        "pe_k2": jnp.concatenate([pe_k, pe_k], axis=-1),
        "pe_v2": jnp.concatenate([pe_v, pe_v], axis=-1),
        "w1k_bd": _block_diag2(w1_k.reshape(CMP_BLOCK, HEAD_DIM, CMP_HIDDEN)).astype(BF16),
        "w1v_bd": _block_diag2(w1_v.reshape(CMP_BLOCK, HEAD_DIM, CMP_HIDDEN)).astype(BF16),
        "w2k_bd": _block_diag2(w2_k).astype(BF16),
        "w2v_bd": _block_diag2(w2_v).astype(BF16),
        "w_out": w_out.astype(BF16),
        "w_pg": w_ple_gate.astype(BF16),
        "w_ple": w_ple.astype(BF16),
    }


def _prompt_layer(x, p, lw):
    b, t, d = x.shape
    x2d = x.reshape(b * t, d)
    qa, sb_rows, gates, qb, nsa_rows, win_rows, br = _project(x2d, lw, tm=256)
    o_sb = _sb_prompt(qa, sb_rows, b, t, tq=256)
    nbp = -(-(t // SEL_BLOCK) // LANES) * LANES
    kc, vc = _cmp_prompt(nsa_rows, lw, b, t, nbp)
    o_nsa = _nsa_prompt(qb, br, kc, vc, nsa_rows, win_rows, b, t, tq=128, nbp=nbp, sel_kc=512)
    y = _output(o_sb, o_nsa, gates, x2d, p.reshape(b * t, -1), lw, tm=256)
    keep = min(WINDOW, t)
    return (y.reshape(b, t, d),
            sb_rows.reshape(b, t, 2, SB_HEADS, HEAD_DIM),
            nsa_rows.reshape(b, t, 4, NSA_KV_HEADS, HEAD_DIM),
            win_rows.reshape(b, t, 2, NSA_KV_HEADS, HEAD_DIM)[:, t - keep:])


def _sample_layer(x, p, cache_sb, cache_nsa, state_win, page_table, lw):
    db, tq, d = x.shape
    n = db * tq
    x2d = x.reshape(n, d)
    qa, sb_rows, gates, qb, nsa_rows, win_rows, br = _project(x2d, lw, tm=n)
    o_sb = _sb_sample(qa, sb_rows, cache_sb, page_table, db, tq, n_pp=8)
    o_nsa = _nsa_sample(qb, br, nsa_rows, win_rows, state_win, cache_nsa, page_table, lw, db, tq,
                        n_pp=8, sel_kc=256)
    y = _output(o_sb, o_nsa, gates, x2d, p.reshape(n, -1), lw, tm=n)
    win_new = win_rows.reshape(db, tq, 2, NSA_KV_HEADS, HEAD_DIM)
    w_buf = state_win.shape[1]
    win_out = jnp.concatenate([state_win, win_new], axis=1)[:, tq:]
    assert win_out.shape[1] == w_buf
    return (y.reshape(db, tq, d),
            sb_rows.reshape(db, tq, 2, SB_HEADS, HEAD_DIM),
            nsa_rows.reshape(db, tq, 4, NSA_KV_HEADS, HEAD_DIM),
            win_out)


def kernel(x_prompt, x_sample, p_prompt, p_sample, cache_sb, cache_nsa, state_win, page_table, norm_g, w_in, q_norm_g, kcmp_norm_g, ksel_norm_g, kwin_norm_g, cmp_pos_k, cmp_w1_k, cmp_w2_k, cmp_pos_v, cmp_w1_v, cmp_w2_v, w_out, w_ple, w_ple_gate):
    depth = norm_g.shape[0]
    assert state_win.shape[2] == WINDOW and x_prompt.shape[1] >= WINDOW + LANES
    y_p, y_s = x_prompt, x_sample
    outs = [[] for _ in range(6)]
    for i in range(depth):
        lw = _layer_weights(norm_g[i], w_in[i], q_norm_g[i], kcmp_norm_g[i], ksel_norm_g[i],
                            kwin_norm_g[i], cmp_pos_k[i], cmp_w1_k[i], cmp_w2_k[i],
                            cmp_pos_v[i], cmp_w1_v[i], cmp_w2_v[i], w_out[i], w_ple[i], w_ple_gate[i])
        y_p, a, b, c = _prompt_layer(y_p, p_prompt[i], lw)
        outs[0].append(a); outs[2].append(b); outs[4].append(c)
        y_s, a, b, c = _sample_layer(y_s, p_sample[i], cache_sb[i], cache_nsa[i], state_win[i],
                                     page_table, lw)
        outs[1].append(a); outs[3].append(b); outs[5].append(c)
    return (y_p, y_s) + tuple(jnp.stack(o) for o in outs)
```

```python
import functools

import numpy as np
import jax
import jax.numpy as jnp
from jax import lax
from jax.experimental import pallas as pl
from jax.experimental.pallas import tpu as pltpu

F32 = jnp.float32
BF16 = jnp.bfloat16

HEAD_DIM = 64
SB_HEADS = 8
NSA_HEADS = 8
NSA_KV_HEADS = 2
NSA_GROUP = NSA_HEADS // NSA_KV_HEADS
CMP_BLOCK = 32
CMP_HIDDEN = 256
SEL_BLOCK = 64
N_SEL = 16
WINDOW = 512
PAGE_SIZE = 128
EPS = 1e-6
FORCED_SCORE = 1e3
NEG = -1e30
NEG_BIG = -2.0 ** 100
LOWEST = -3e38
LOG2E = float(np.log2(np.e))
QSCALE = HEAD_DIM ** -0.5 * LOG2E
SB_W = SB_HEADS * HEAD_DIM
NSA_W = NSA_HEADS * HEAD_DIM
KV_W = NSA_KV_HEADS * HEAD_DIM
LANES = 128
SUBLANES = 8
NEVER_READABLE = 2 ** 30
SLOPES = [float(np.power(np.float32(2.0), np.float32(-8.0 * h / NSA_HEADS)))
          for h in range(1, NSA_HEADS + 1)]
N_SLOTS = WINDOW // LANES + 1
N_FEAT = 6
VALID_LANE = N_FEAT

C_QA, C_SBK, C_GA, C_QB, C_KC, C_KS, C_VS, C_KW, C_VW, C_BR = (
    0, SB_W, 3 * SB_W, 4 * SB_W, 4 * SB_W + NSA_W, 4 * SB_W + NSA_W + 2 * KV_W,
    4 * SB_W + NSA_W + 3 * KV_W, 4 * SB_W + NSA_W + 4 * KV_W, 4 * SB_W + NSA_W + 5 * KV_W,
    4 * SB_W + NSA_W + 6 * KV_W)
C_GB = C_BR + 3 * NSA_HEADS

VMEM_LIMIT = 56 * 1024 * 1024


def _cparams(n_grid_dims):
    return pltpu.CompilerParams(dimension_semantics=("arbitrary",) * n_grid_dims,
                                vmem_limit_bytes=VMEM_LIMIT)


def _bf16_parts(x, n):
    parts, r = [], np.float64(x)
    for _ in range(n):
        p = np.float64(np.asarray(r, np.float32).astype(jnp.bfloat16).astype(np.float32))
        parts.append(float(p))
        r -= p
    return parts


L2E_PARTS = _bf16_parts(LOG2E, N_FEAT // 2)


def _dot(a, b):
    return jnp.dot(a, b, preferred_element_type=F32)


def _dot_nt(a, b):
    return lax.dot_general(a, b, (((1,), (1,)), ((), ())), preferred_element_type=F32)


def _iota(shape, dim):
    return lax.broadcasted_iota(jnp.int32, shape, dim)


def _div(x, n):
    assert n & (n - 1) == 0
    return lax.shift_right_logical(x, jnp.int32(n.bit_length() - 1))


def _mod(x, n):
    assert n & (n - 1) == 0
    return lax.bitwise_and(x, jnp.int32(n - 1))


def _pair_rms(x, g2):
    lo = _iota(x.shape, 1) < HEAD_DIM
    x2 = x * x
    s0 = jnp.sum(jnp.where(lo, x2, 0.0), axis=-1, keepdims=True)
    s1 = jnp.sum(jnp.where(lo, 0.0, x2), axis=-1, keepdims=True)
    r = jnp.where(lo, lax.rsqrt(s0 * (1.0 / HEAD_DIM) + EPS), lax.rsqrt(s1 * (1.0 / HEAD_DIM) + EPS))
    return x * r * g2


def _proj_kernel(x_ref, ng_ref, wm_ref, wbr_ref, wgb_ref, qg_ref, ksg_ref, kwg_ref,
                 qa_ref, sb_ref, gates_ref, qb_ref, nsa_ref, win_ref, br_ref, *t_refs):
    x = x_ref[...]
    ms = jnp.mean(x * x, axis=-1, keepdims=True)
    h = (x * lax.rsqrt(ms + EPS) * ng_ref[...]).astype(BF16)

    def z(lo, hi):
        return _dot(h, wm_ref[:, lo:hi])

    def put(ref, t_ref, c0, val):
        ref[:, c0:c0 + val.shape[1]] = val
        if t_ref is not None:
            for c in range(0, val.shape[1], LANES):
                t_ref[c0 + c:c0 + c + LANES, :] = val[:, c:c + LANES].T

    sbt_ref, nsat_ref, wint_ref = t_refs if t_refs else (None, None, None)
    qa_ref[...] = z(C_QA, C_SBK) * QSCALE
    put(sb_ref, sbt_ref, 0, z(C_SBK, C_GA))
    ga = z(C_GA, C_QB)
    gates_ref[:, 0:SB_W] = ga * jax.nn.sigmoid(ga)
    gb = _dot(h, wgb_ref[...])
    gates_ref[:, SB_W:SB_W + NSA_W] = gb * jax.nn.sigmoid(gb)
    qb = z(C_QB, C_KC)
    for pr in range(NSA_W // LANES):
        sl = slice(pr * LANES, (pr + 1) * LANES)
        qb_ref[:, sl] = _pair_rms(qb[:, sl], qg_ref[...]) * QSCALE
    put(nsa_ref, nsat_ref, 0, z(C_KC, C_KS))
    put(nsa_ref, nsat_ref, 2 * KV_W, _pair_rms(z(C_KS, C_VS), ksg_ref[...]))
    put(nsa_ref, nsat_ref, 3 * KV_W, z(C_VS, C_KW))
    put(win_ref, wint_ref, 0, _pair_rms(z(C_KW, C_VW), kwg_ref[...]))
    put(win_ref, wint_ref, KV_W, z(C_VW, C_BR))
    br_ref[...] = jax.nn.sigmoid(_dot(h, wbr_ref[...]))


def _project(x2d, lw, tm, seq=None):
    n, d = x2d.shape
    full = lambda a: pl.BlockSpec(a.shape, lambda i: (0,) * a.ndim)
    row = lambda w: pl.BlockSpec((tm, w), lambda i: (i, 0))
    ins = [x2d, lw["norm_g"], lw["w_main"], lw["w_br"], lw["w_gb"], lw["q_g2"], lw["ks_g2"], lw["kw_g2"]]
    widths = [SB_W, 2 * SB_W, SB_W + NSA_W, NSA_W, 4 * KV_W, 2 * KV_W, LANES]
    out_specs = [row(w) for w in widths]
    out_shape = [jax.ShapeDtypeStruct((n, w), F32) for w in widths]
    if seq is not None:
        nq = seq // tm
        for w in (2 * SB_W, 4 * KV_W, 2 * KV_W):
            out_specs.append(pl.BlockSpec((None, w, tm), lambda i: (i // nq, 0, i % nq)))
            out_shape.append(jax.ShapeDtypeStruct((n // seq, w, seq), F32))
    return pl.pallas_call(
        _proj_kernel,
        grid=(n // tm,),
        in_specs=[row(d)] + [full(a) for a in ins[1:]],
        out_specs=out_specs,
        out_shape=out_shape,
        compiler_params=_cparams(1),
        name="proj",
    )(*ins)


def _out_kernel(osb_ref, onsa_ref, gates_ref, x_ref, p_ref, wo_ref, wpg_ref, wp_ref, y_ref):
    ua = (osb_ref[...] * gates_ref[:, 0:SB_W]).astype(BF16)
    ub = (onsa_ref[...] * gates_ref[:, SB_W:SB_W + NSA_W]).astype(BF16)
    h = x_ref[...] + (_dot(ua, wo_ref[0:SB_W, :]) + _dot(ub, wo_ref[SB_W:SB_W + NSA_W, :]))
    gate = jax.nn.sigmoid(_dot(h.astype(BF16), wpg_ref[...]))
    y_ref[...] = h + gate * _dot(p_ref[...].astype(BF16), wp_ref[...])


def _output(osb, onsa, gates, x2d, p2d, lw, tm):
    n, d = x2d.shape
    full = lambda a: pl.BlockSpec(a.shape, lambda i: (0,) * a.ndim)
    row = lambda w: pl.BlockSpec((tm, w), lambda i: (i, 0))
    ins = [osb, onsa, gates, x2d, p2d, lw["w_out"], lw["w_pg"], lw["w_ple"]]
    return pl.pallas_call(
        _out_kernel,
        grid=(n // tm,),
        in_specs=[row(a.shape[1]) for a in ins[:5]] + [full(a) for a in ins[5:]],
        out_specs=row(d),
        out_shape=jax.ShapeDtypeStruct((n, d), F32),
        compiler_params=_cparams(1),
        name="outproj",
    )(*ins)


def _sb_streams(zs, vs, rs, u_bf, befores, av_fn=_dot):
    pairs = [(i, c) for i in range(len(zs)) for c in range(len(zs[i]))]
    lss, lks, css, sums = {}, {}, {}, {}
    for i, c in pairs:
        z = zs[i][c]
        sp = jnp.log2(1.0 + jnp.exp2(-jnp.abs(z)))
        ls = jnp.minimum(z, 0.0) - sp
        lk = ls - z
        if befores[i][c] is not None:
            ls = jnp.where(befores[i][c], ls, NEG)
            lk = jnp.where(befores[i][c], lk, 0.0)
        lss[i, c], lks[i, c] = ls, lk
    for i, c in pairs:
        css[i, c] = _dot(lks[i, c].astype(BF16), u_bf)
    for i, c in pairs:
        sums[i, c] = jnp.sum(lks[i, c], axis=-1, keepdims=True)
    r_in = {}
    r_out = []
    for i in range(len(zs)):
        r = rs[i]
        for c in range(len(zs[i])):
            r_in[i, c] = r
            r = r + sums[i, c]
        r_out.append(r)
    contribs = [None] * len(zs)
    for i, c in pairs:
        a = jnp.exp2(lss[i, c] + css[i, c])
        t = av_fn(a.astype(BF16), vs[c]) * jnp.exp2(r_in[i, c])
        contribs[i] = t if contribs[i] is None else contribs[i] + t
    return contribs, r_out


def _suffix_matrix(n):
    j = np.arange(n)
    return jnp.asarray((j[:, None] > j[None, :]).astype(np.float32), dtype=BF16)


def _sb_prompt_tile(i, q, k_ref, v_ref, u, tq):
    lane = _iota(q.shape, 1)
    before = _iota((tq, tq), 1) < _iota((tq, tq), 0)
    qms = [jnp.concatenate([jnp.where(lane < HEAD_DIM, q, 0.0), jnp.where(lane < HEAD_DIM, 0.0, q)],
                           axis=0).astype(BF16)]
    befores = [jnp.concatenate([before, before], axis=0)]
    n, rh = 1, 2 * tq

    def sweep(chunks, carry, masks):
        accs, rs = carry
        starts = [c * tq if isinstance(c, int) else pl.multiple_of(c * tq, tq) for c in chunks]
        kcs = [k_ref[pl.ds(c0, tq), :].astype(BF16) for c0 in starts]
        vcs = [v_ref[pl.ds(c0, tq), :].astype(BF16) for c0 in starts]
        zs = [[_dot_nt(qm, kc) for kc in kcs] for qm in qms]
        contribs, rs2 = _sb_streams(zs, vcs, list(rs), u, masks)
        return tuple(a + c for a, c in zip(accs, contribs)), tuple(rs2)

    carry = (tuple(jnp.zeros((rh, LANES), F32) for _ in range(n)),
             tuple(jnp.zeros((rh, 1), F32) for _ in range(n)))
    none1, none2 = [[None]] * n, [[None, None]] * n
    carry = lax.cond(i >= 1,
                     lambda c: sweep([i, i - 1], c, [[m, None] for m in befores]),
                     lambda c: sweep([i], c, [[m] for m in befores]), carry)
    carry = lax.fori_loop(
        0, jnp.maximum(i - 1, 0) // 2,
        lambda j, c: sweep([i - 2 - 2 * j, i - 3 - 2 * j], c, none2), carry)
    accs, _ = lax.cond(jnp.logical_and(i >= 2, i % 2 == 0),
                       lambda c: sweep([0], c, none1), lambda c: c, carry)
    acc_all = jnp.concatenate(accs, axis=0)
    return jnp.where(lane < HEAD_DIM, acc_all[0:tq], acc_all[tq:2 * tq])


def _sb_prompt_kernel(q_ref, k_ref, v_ref, u_ref, o_ref, *, tq):
    u = u_ref[...]

    def tile(i, carry):
        r0 = pl.multiple_of(i * tq, tq)
        o_ref[pl.ds(r0, tq), :] = _sb_prompt_tile(i, q_ref[pl.ds(r0, tq), :], k_ref, v_ref, u, tq)
        return carry

    lax.fori_loop(0, q_ref.shape[0] // tq, tile, 0)


def _sb_prompt(qa, sb_rows, b, t, tq):
    npair = SB_W // LANES
    return pl.pallas_call(
        functools.partial(_sb_prompt_kernel, tq=tq),
        grid=(b, npair),
        in_specs=[
            pl.BlockSpec((t, LANES), lambda bi, hp: (bi, hp)),
            pl.BlockSpec((t, LANES), lambda bi, hp: (bi, hp)),
            pl.BlockSpec((t, LANES), lambda bi, hp: (bi, npair + hp)),
            pl.BlockSpec((tq, tq), lambda bi, hp: (0, 0)),
        ],
        out_specs=pl.BlockSpec((t, LANES), lambda bi, hp: (bi, hp)),
        out_shape=jax.ShapeDtypeStruct((b * t, SB_W), F32),
        compiler_params=_cparams(2),
        name="sb_prompt",
    )(qa, sb_rows, sb_rows, _suffix_matrix(tq))


def _pages_last(cache):
    pool, ps, a, b, d = cache.shape
    return jnp.transpose(cache, (0, 2, 3, 4, 1)).reshape(pool, a, b * d, ps)


def _sb_sample_kernel(pt_ref, q_ref, new_ref, *rest, n_pp, n_steps, tq):
    del pt_ref
    page_refs = rest[:n_pp]
    u_ref, o_ref, acc_ref, r_ref = rest[n_pp:]
    s = pl.program_id(1)
    rows = SB_HEADS * tq
    q = q_ref[...]
    bd = _div(_iota((rows, SB_W), 0), tq) == _div(_iota((rows, SB_W), 1), HEAD_DIM)
    qbd = jnp.where(bd, jnp.concatenate([q] * SB_HEADS, axis=0), 0.0).astype(BF16)
    u = u_ref[...]

    def do_pages(kts, vts, befores):
        zs = [[_dot(qbd, kt) for kt in kts]]
        contribs, r2 = _sb_streams(zs, vts, [r_ref[...]], u, [befores], av_fn=_dot_nt)
        acc_ref[...] = acc_ref[...] + contribs[0]
        r_ref[...] = r2[0]

    @pl.when(s == 0)
    def _():
        acc_ref[...] = jnp.zeros_like(acc_ref)
        r_ref[...] = jnp.zeros_like(r_ref)
        kv = jnp.concatenate([new_ref[...], jnp.zeros((PAGE_SIZE - tq, 2 * SB_W), F32)], axis=0)
        kt = jnp.concatenate([kv[:, c0:c0 + LANES].T for c0 in range(0, SB_W, LANES)], axis=0)
        vt = jnp.concatenate([kv[:, c0:c0 + LANES].T for c0 in range(SB_W, 2 * SB_W, LANES)], axis=0)
        t_loc = _mod(_iota((rows, PAGE_SIZE), 0), tq)
        do_pages([kt.astype(BF16)], [vt.astype(BF16)], [_iota((rows, PAGE_SIZE), 1) < t_loc])

    @pl.when(s > 0)
    def _():
        do_pages([page_refs[j][0].astype(BF16) for j in range(n_pp)],
                 [page_refs[j][1].astype(BF16) for j in range(n_pp)], [None] * n_pp)

    @pl.when(s == n_steps - 1)
    def _():
        accm = jnp.where(bd, acc_ref[...], 0.0)
        o = accm[0:tq]
        for h in range(1, SB_HEADS):
            o = o + accm[h * tq:(h + 1) * tq]
        o_ref[...] = o


def _sb_sample(qa, sb_new, cache_sb, page_table, db, tq, n_pp):
    n_pages = page_table.shape[1]
    n_steps = n_pages // n_pp + 1
    assert (n_steps - 1) * n_pp == n_pages and cache_sb.shape[1:] == (PAGE_SIZE, 2, SB_HEADS, HEAD_DIM)
    cache_t = _pages_last(cache_sb)

    def page_spec(j):
        def imap(bi, s, pt):
            lp = n_pages - 1 - (jnp.maximum(s - 1, 0) * n_pp + j)
            return (pt[bi, lp], 0, 0, 0)
        return pl.BlockSpec((None, 2, SB_W, PAGE_SIZE), imap)

    grid_spec = pltpu.PrefetchScalarGridSpec(
        num_scalar_prefetch=1,
        grid=(db, n_steps),
        in_specs=[pl.BlockSpec((tq, SB_W), lambda bi, s, pt: (bi, 0)),
                  pl.BlockSpec((tq, 2 * SB_W), lambda bi, s, pt: (bi, 0))]
                 + [page_spec(j) for j in range(n_pp)]
                 + [pl.BlockSpec((PAGE_SIZE, PAGE_SIZE), lambda bi, s, pt: (0, 0))],
        out_specs=pl.BlockSpec((tq, SB_W), lambda bi, s, pt: (bi, 0)),
        scratch_shapes=[pltpu.VMEM((SB_HEADS * tq, SB_W), F32),
                        pltpu.VMEM((SB_HEADS * tq, 1), F32)],
    )
    return pl.pallas_call(
        functools.partial(_sb_sample_kernel, n_pp=n_pp, n_steps=n_steps, tq=tq),
        grid_spec=grid_spec,
        out_shape=jax.ShapeDtypeStruct((db * tq, SB_W), F32),
        compiler_params=_cparams(2),
        name="sb_sample",
    )(page_table, qa, sb_new, *([cache_t] * n_pp), _suffix_matrix(PAGE_SIZE))


def _compress(get_xk, get_xv, m, pek_ref, pev_ref, w1k_ref, w1v_ref, w2k_ref, w2v_ref, kcg_ref):
    hk = jnp.zeros((m + SUBLANES, NSA_KV_HEADS * CMP_HIDDEN), F32)
    hv = jnp.zeros((m + SUBLANES, NSA_KV_HEADS * CMP_HIDDEN), F32)

    def lhs(get_x, pe_ref, p):
        pe = jnp.broadcast_to(pe_ref[p:p + 1, :], (SUBLANES, KV_W))
        return jnp.concatenate([get_x(p), pe], axis=0).astype(BF16)

    for pp in range(CMP_BLOCK // 2):
        xk = jnp.concatenate([lhs(get_xk, pek_ref, 2 * pp), lhs(get_xk, pek_ref, 2 * pp + 1)], axis=1)
        xv = jnp.concatenate([lhs(get_xv, pev_ref, 2 * pp), lhs(get_xv, pev_ref, 2 * pp + 1)], axis=1)
        hk = hk + _dot(xk, w1k_ref[pp])
        hv = hv + _dot(xv, w1v_ref[pp])
    hk = hk[0:m] + hk[m:m + 1]
    hv = hv[0:m] + hv[m:m + 1]
    kc = _dot((hk * jax.nn.sigmoid(hk)).astype(BF16), w2k_ref[...])
    vc = _dot((hv * jax.nn.sigmoid(hv)).astype(BF16), w2v_ref[...])
    return _pair_rms(kc, kcg_ref[...]), vc


def _cmp_prompt_kernel(xk_ref, xv_ref, pek_ref, pev_ref, w1k_ref, w1v_ref, w2k_ref, w2v_ref, kcg_ref,
                       kc_ref, vc_ref, *, m):
    kc, vc = _compress(lambda p: xk_ref[pl.ds(p, m, stride=CMP_BLOCK), :],
                       lambda p: xv_ref[pl.ds(p, m, stride=CMP_BLOCK), :],
                       m, pek_ref, pev_ref, w1k_ref, w1v_ref, w2k_ref, w2v_ref, kcg_ref)
    pad = jnp.zeros((kc_ref.shape[0] - m, KV_W), F32)
    kc_ref[...] = jnp.concatenate([kc, pad], axis=0)
    vc_ref[...] = jnp.concatenate([vc, pad], axis=0)


def _cmp_weights(lw):
    return [lw["pe_k2"], lw["pe_v2"], lw["w1k_bd"], lw["w1v_bd"], lw["w2k_bd"], lw["w2v_bd"], lw["kc_g2"]]


def _cmp_prompt(nsa_rows, lw, b, t, n_rows):
    m = t // CMP_BLOCK
    ws = _cmp_weights(lw)
    full = lambda a: pl.BlockSpec(a.shape, lambda bi: (0,) * a.ndim)
    return pl.pallas_call(
        functools.partial(_cmp_prompt_kernel, m=m),
        grid=(b,),
        in_specs=[pl.BlockSpec((t, KV_W), lambda bi: (bi, 0)),
                  pl.BlockSpec((t, KV_W), lambda bi: (bi, 1))] + [full(a) for a in ws],
        out_specs=[pl.BlockSpec((None, n_rows, KV_W), lambda bi: (bi, 0, 0))] * 2,
        out_shape=[jax.ShapeDtypeStruct((b, n_rows, KV_W), F32)] * 2,
        compiler_params=_cparams(1),
        name="cmp_prompt",
    )(nsa_rows, nsa_rows, *ws)


def _pos_features(n_keys, width, first_lane):
    f = np.zeros((n_keys, width), np.float32)
    k = np.arange(n_keys)
    for j in range(N_FEAT // 2):
        f[:, first_lane + j] = k // SEL_BLOCK
        f[:, first_lane + N_FEAT // 2 + j] = k % SEL_BLOCK
    return f


def _query_features(width, first_lane, valid_lane=None):
    f = np.zeros((NSA_HEADS, width), np.float32)
    for h in range(NSA_HEADS):
        for j, c in enumerate(L2E_PARTS):
            f[h, first_lane + j] = SEL_BLOCK * SLOPES[h] * c
            f[h, first_lane + N_FEAT // 2 + j] = SLOPES[h] * c
        if valid_lane is not None:
            f[h, valid_lane] = 1.0
    return f


def _sel_key_features(n_keys, nbp):
    f = _pos_features(n_keys, nbp, nbp - N_FEAT)
    k = np.arange(n_keys)
    assert k.max() // SEL_BLOCK < nbp - N_FEAT
    f[k, k // SEL_BLOCK] = 1.0
    return jnp.asarray(f, dtype=BF16)


def _topk_roll(score, n_top):
    n = score.shape[1]
    lane = _iota(score.shape, 1)
    rank = jnp.zeros(score.shape, F32)
    for k in range(1, n):
        r = pltpu.roll(score, k, 1)
        rank = rank + jnp.where(lane >= k, jnp.where(r >= score, 1.0, 0.0),
                                jnp.where(r > score, 1.0, 0.0))
    return jnp.where(rank < n_top, 1.0, 0.0)


def _topk_rank(score, n_blk, n_top):
    rows, lanes = score.shape
    nb8 = -(-n_blk // SUBLANES) * SUBLANES
    st = jnp.concatenate([score[r0:r0 + lanes].T[0:nb8] for r0 in range(0, rows, lanes)], axis=1)
    bi = _iota(st.shape, 0)
    rank = jnp.zeros(st.shape, F32)
    for j in range(n_blk):
        row = st[j:j + 1, :]
        rank = rank + jnp.where(bi > j, jnp.where(row >= st, 1.0, 0.0), jnp.where(row > st, 1.0, 0.0))
    selt = jnp.where(rank < n_top, 1.0, 0.0)
    selt = jnp.concatenate([selt, jnp.zeros((lanes - nb8, rows), F32)], axis=0)
    return jnp.concatenate([selt[:, r0:r0 + lanes].T for r0 in range(0, rows, lanes)], axis=0)


def _nsa_attend(q_ref, br_ref, cmp_pieces, imp_fn, sel_ref, sel_cols, pose_ref, fq_sel_ref,
                win_ref, win_starts, win_invalid, featw_ref, fq_win_ref, o_ref, *,
                t0, tq, nbp, n_blk, sel_kc, topk_fn):
    nh, rg = NSA_HEADS, NSA_GROUP
    rows = nh * tq
    grows = rg * tq
    r_loc = _mod(_iota((rows, 1), 0), tq)
    tcol = t0 + r_loc
    tq_col = t0 + _iota((tq, 1), 0)
    lane_q = _iota((tq, LANES), 1)
    br = br_ref[...]

    parts, slopes = [], []
    for h in range(nh):
        g = h // rg
        qp = q_ref[:, (h // 2) * LANES:(h // 2 + 1) * LANES]
        if (h % 2) != g:
            qp = pltpu.roll(qp, HEAD_DIM, 1)
        keep = (lane_q < HEAD_DIM) if g == 0 else (lane_q >= HEAD_DIM)
        parts.append(jnp.where(keep, qp, 0.0))
        slopes.append(jnp.full((tq, 1), SLOPES[h] * LOG2E, F32))
    qs = jnp.concatenate(parts, axis=0).astype(BF16)
    slope = jnp.concatenate(slopes, axis=0)

    def head_rows(x):
        w = x.shape[1]
        return jnp.concatenate([jnp.broadcast_to(x[h:h + 1, :], (tq, w)) for h in range(nh)], axis=0)

    lane_w = _iota((LANES, LANES), 1)
    kws, vws = [], []
    for sl in range(N_SLOTS):
        kw = win_ref[pl.ds(win_starts[sl], LANES), 0:KV_W].astype(BF16)
        fw = featw_ref[sl * LANES:(sl + 1) * LANES, :]
        if win_invalid is not None and sl < N_SLOTS - 1:
            fw = jnp.where(jnp.logical_and(lane_w == VALID_LANE, win_invalid[sl]),
                           jnp.asarray(NEG_BIG, BF16), fw)
        kws.append(jnp.concatenate([kw, fw], axis=1))
        vws.append(win_ref[pl.ds(win_starts[sl], LANES), KV_W:2 * KV_W].astype(BF16))
    q_win = jnp.concatenate([qs, head_rows(fq_win_ref[...]).astype(BF16)], axis=1)
    s = _dot_nt(q_win, jnp.concatenate(kws, axis=0))
    c_loc = _iota((1, LANES), 1)
    s = jnp.concatenate([jnp.where(c_loc > r_loc, s[:, 0:LANES], NEG),
                         s[:, LANES:(N_SLOTS - 1) * LANES],
                         jnp.where(c_loc <= r_loc, s[:, (N_SLOTS - 1) * LANES:], NEG)], axis=1)
    p = jnp.exp2(s - jnp.max(s, axis=-1, keepdims=True))
    o_w = _dot(p.astype(BF16), jnp.concatenate(vws, axis=0)) * (1.0 / jnp.sum(p, axis=-1, keepdims=True))

    s_list, m_list = [], []
    mx = None
    for kx, _, c_end in cmp_pieces:
        s = _dot_nt(qs, kx) - slope * (tcol - c_end).astype(F32)
        mk = c_end <= tcol
        s = jnp.where(mk, s, NEG)
        c = jnp.max(s, axis=-1, keepdims=True)
        mx = c if mx is None else jnp.maximum(mx, c)
        s_list.append(s)
        m_list.append(mk)
    es = [jnp.where(mk, jnp.exp2(s - mx), 0.0) for s, mk in zip(s_list, m_list)]
    den = None
    for e in es:
        c = jnp.sum(e, axis=-1, keepdims=True)
        den = c if den is None else den + c
    inv = 1.0 / jnp.where(den > 0, den, 1.0)
    o_c, pp = None, None
    for e, (_, vx, _) in zip(es, cmp_pieces):
        p = e * inv
        c = _dot(p.astype(BF16), vx)
        o_c = c if o_c is None else o_c + c
        pp = p if pp is None else pp + p

    blk = _iota((tq, nbp), 1)
    cur = _div(tq_col, SEL_BLOCK)
    scores = []
    for g in range(NSA_KV_HEADS):
        tot = pp[g * grows:g * grows + tq]
        for r in range(1, rg):
            tot = tot + pp[g * grows + r * tq:g * grows + (r + 1) * tq]
        imp = imp_fn(tot)
        score = jnp.where(blk > cur, -1.0, imp)
        score = jnp.where(blk == 0, FORCED_SCORE, score)
        score = jnp.where(blk == cur, FORCED_SCORE, score)
        score = jnp.where(blk == cur - 1, FORCED_SCORE, score)
        scores.append(jnp.where(blk < n_blk, score, LOWEST))
    sel = topk_fn(jnp.concatenate(scores, axis=0))

    blk0 = _div(t0, SEL_BLOCK) if not isinstance(t0, int) else t0 // SEL_BLOCK
    fq_sel = head_rows(fq_sel_ref[...])
    biases = []
    for g in range(NSA_KV_HEADS):
        sg = sel[g * tq:(g + 1) * tq]
        bias = jnp.where(blk < blk0, jnp.where(sg > 0.5, 0.0, NEG_BIG), NEG_BIG)
        bias = jnp.where(blk < n_blk, bias, 0.0)
        biases.extend([bias] * rg)
    q_sel = jnp.concatenate([qs, (jnp.concatenate(biases, axis=0) + fq_sel).astype(BF16)], axis=1)
    q_diag = jnp.concatenate([qs, fq_sel.astype(BF16)], axis=1)

    def flash_step(carry, q_aug, start, n_keys, mask):
        m, l, acc = carry
        k = sel_ref[pl.ds(start, n_keys), sel_cols[0]:sel_cols[0] + KV_W].astype(BF16)
        v = sel_ref[pl.ds(start, n_keys), sel_cols[1]:sel_cols[1] + KV_W].astype(BF16)
        k_aug = jnp.concatenate([k, pose_ref[pl.ds(start, n_keys), :]], axis=1)
        s = _dot_nt(q_aug, k_aug)
        if mask is not None:
            s = jnp.where(mask, s, NEG)
        m_new = jnp.maximum(m, jnp.max(s, axis=-1, keepdims=True))
        alpha = jnp.exp2(m - m_new)
        p = jnp.exp2(s - m_new)
        l = alpha * l + jnp.sum(p, axis=-1, keepdims=True)
        acc = alpha * acc + _dot(p.astype(BF16), v)
        return m_new, l, acc

    carry = (jnp.full((rows, 1), NEG, F32), jnp.zeros((rows, 1), F32), jnp.zeros((rows, KV_W), F32))
    carry = lax.fori_loop(
        0, (t0 + sel_kc - 1) // sel_kc,
        lambda j, c: flash_step(c, q_sel, pl.multiple_of(j * sel_kc, sel_kc), sel_kc, None), carry)
    causal = _iota((1, LANES), 1) <= r_loc
    d0 = t0 if isinstance(t0, int) else pl.multiple_of(t0, LANES)
    _, l_s, acc_s = flash_step(carry, q_diag, d0, LANES, causal)
    o_s = acc_s * (1.0 / l_s)

    def gate_col(j):
        return jnp.concatenate([br[:, h * 3 + j:h * 3 + j + 1] for h in range(nh)], axis=0)

    o = gate_col(0) * o_c + gate_col(1) * o_s + gate_col(2) * o_w
    for pr in range(nh // 2):
        pieces = []
        for h in (2 * pr, 2 * pr + 1):
            piece = o[h * tq:(h + 1) * tq]
            if (h % 2) != h // rg:
                piece = pltpu.roll(piece, HEAD_DIM, 1)
            pieces.append(piece)
        o_ref[:, pr * LANES:(pr + 1) * LANES] = jnp.where(lane_q < HEAD_DIM, pieces[0], pieces[1])


def _nsa_prompt_kernel(q_ref, br_ref, kc_ref, vc_ref, sel_ref, win_ref, pose_ref, fqs_ref,
                       featw_ref, fqw_ref, o_ref, *, tq, nbp, n_blk, sel_kc):
    kx = jnp.concatenate([kc_ref[pl.ds(0, n_blk, stride=2), :], kc_ref[pl.ds(1, n_blk, stride=2), :],
                          jnp.zeros((LANES - 2 * n_blk, KV_W), F32)], axis=0).astype(BF16)
    vx = jnp.concatenate([vc_ref[pl.ds(0, n_blk, stride=2), :], vc_ref[pl.ds(1, n_blk, stride=2), :],
                          jnp.zeros((LANES - 2 * n_blk, KV_W), F32)], axis=0).astype(BF16)
    ci = _iota((1, LANES), 1)
    c_end = jnp.where(ci < n_blk, ci * (2 * CMP_BLOCK) + (CMP_BLOCK - 1),
                      (ci - n_blk) * (2 * CMP_BLOCK) + (2 * CMP_BLOCK - 1))
    c_end = jnp.where(ci < 2 * n_blk, c_end, jnp.int32(NEVER_READABLE))

    def imp_fn(tot):
        return tot + pltpu.roll(tot, LANES - n_blk, 1)

    def tile(i, carry):
        t0 = pl.multiple_of(i * tq, tq)
        starts = [pl.multiple_of(jnp.maximum(t0 - WINDOW + sl * LANES, 0), LANES) for sl in range(N_SLOTS)]
        invalid = [i < (N_SLOTS - 1 - sl) for sl in range(N_SLOTS - 1)]
        rows = pl.ds(t0, tq)
        _nsa_attend(q_ref.at[rows], br_ref.at[rows], [(kx, vx, c_end)], imp_fn, sel_ref, (0, KV_W),
                    pose_ref, fqs_ref, win_ref, starts, invalid, featw_ref, fqw_ref, o_ref.at[rows],
                    t0=t0, tq=tq, nbp=nbp, n_blk=n_blk, sel_kc=sel_kc,
                    topk_fn=functools.partial(_topk_rank, n_blk=n_blk, n_top=min(N_SEL, n_blk)))
        return carry

    lax.fori_loop(0, q_ref.shape[0] // tq, tile, 0)


def _win_constants():
    featw = _pos_features(N_SLOTS * LANES, LANES, 0)
    return jnp.asarray(featw, dtype=BF16), jnp.asarray(_query_features(LANES, 0, VALID_LANE))


def _nsa_prompt(qb, br, kc, vc, nsa_rows, win_rows, b, t, tq, sel_kc):
    n_blk = t // SEL_BLOCK
    nbp = LANES
    assert tq == LANES and 2 * n_blk <= LANES and n_blk % 8 == 0 and t % sel_kc == 0
    pose = _sel_key_features(t, nbp)
    fqs = jnp.asarray(_query_features(nbp, nbp - N_FEAT))
    featw, fqw = _win_constants()
    full = lambda a: pl.BlockSpec(a.shape, lambda bi: (0,) * a.ndim)
    return pl.pallas_call(
        functools.partial(_nsa_prompt_kernel, tq=tq, nbp=nbp, n_blk=n_blk, sel_kc=sel_kc),
        grid=(b,),
        in_specs=[
            pl.BlockSpec((t, NSA_W), lambda bi: (bi, 0)),
            pl.BlockSpec((t, LANES), lambda bi: (bi, 0)),
            pl.BlockSpec((None, kc.shape[1], KV_W), lambda bi: (bi, 0, 0)),
            pl.BlockSpec((None, vc.shape[1], KV_W), lambda bi: (bi, 0, 0)),
            pl.BlockSpec((t, 2 * KV_W), lambda bi: (bi, 1)),
            pl.BlockSpec((t, 2 * KV_W), lambda bi: (bi, 0)),
            full(pose), full(fqs), full(featw), full(fqw),
        ],
        out_specs=pl.BlockSpec((t, NSA_W), lambda bi: (bi, 0)),
        out_shape=jax.ShapeDtypeStruct((b * t, NSA_W), F32),
        compiler_params=_cparams(1),
        name="nsa_prompt",
    )(qb, br, kc, vc, nsa_rows, win_rows, pose, fqs, featw, fqw)


def _nsa_sample_kernel(pt_ref, q_ref, br_ref, new_ref, wnew_ref, wpast_ref,
                       pek_ref, pev_ref, w1k_ref, w1v_ref, w2k_ref, w2v_ref, kcg_ref,
                       pose_ref, fqs_ref, featw_ref, fqw_ref, perm_ref, *rest,
                       n_pp, n_steps, tq, past, m_c, nbp, n_blk, sel_kc):
    del pt_ref
    page_refs = rest[:n_pp]
    o_ref, xs, buf, kcs, vcs, wbuf = rest[n_pp:]
    s = pl.program_id(1)
    bpp = PAGE_SIZE // CMP_BLOCK
    pages = [jnp.concatenate([page_refs[j][c].T for c in range(4)], axis=1) for j in range(n_pp)]
    for j in range(n_pp):
        r0 = pl.multiple_of((s * n_pp + j) * PAGE_SIZE, PAGE_SIZE)
        buf[pl.ds(r0, PAGE_SIZE), :] = pages[j][:, 2 * KV_W:4 * KV_W].astype(BF16)
    for j in range(0, n_pp, 2):
        x2 = jnp.concatenate([pages[j][:, 0:2 * KV_W], pages[j + 1][:, 0:2 * KV_W]],
                             axis=0).astype(BF16)
        y = _dot(perm_ref[...], x2)
        c0 = pl.multiple_of((s * n_pp + j) * bpp, 2 * bpp)
        for p in range(CMP_BLOCK):
            xs[p, pl.ds(c0, 2 * bpp), :] = y[p * 2 * bpp:(p + 1) * 2 * bpp, :]

    @pl.when(s == n_steps - 1)
    def _():
        buf[past:, :] = jnp.concatenate(
            [new_ref[:, 2 * KV_W:4 * KV_W], jnp.zeros((buf.shape[0] - past - tq, 2 * KV_W), F32)],
            axis=0).astype(BF16)
        cb = past // CMP_BLOCK
        xs[:, cb:, :] = jnp.zeros((CMP_BLOCK, m_c - cb, 2 * KV_W), F32)
        for p in range(tq):
            xs[p, cb:cb + 1, :] = new_ref[p:p + 1, 0:2 * KV_W]
        kc, vc = _compress(lambda p: xs[p, :, 0:KV_W], lambda p: xs[p, :, KV_W:2 * KV_W],
                           m_c, pek_ref, pev_ref, w1k_ref, w1v_ref, w2k_ref, w2v_ref, kcg_ref)
        pad = jnp.zeros((kcs.shape[0] - m_c, KV_W), F32)
        kcs[...] = jnp.concatenate([kc, pad], axis=0)
        vcs[...] = jnp.concatenate([vc, pad], axis=0)
        wbuf[0:WINDOW, :] = wpast_ref[...]
        wbuf[WINDOW:WINDOW + tq, :] = wnew_ref[...]
        wbuf[WINDOW + tq:, :] = jnp.zeros((wbuf.shape[0] - WINDOW - tq, 2 * KV_W), F32)
        ci = _iota((1, nbp), 1)
        pieces = []
        for par in range(2):
            pieces.append((kcs[pl.ds(par, nbp, stride=2), :].astype(BF16),
                           vcs[pl.ds(par, nbp, stride=2), :].astype(BF16),
                           ci * (2 * CMP_BLOCK) + (par * CMP_BLOCK + CMP_BLOCK - 1)))
        _nsa_attend(q_ref, br_ref, pieces, lambda tot: tot, buf, (0, KV_W), pose_ref, fqs_ref,
                    wbuf, [sl * LANES for sl in range(N_SLOTS)], None, featw_ref, fqw_ref, o_ref,
                    t0=past, tq=tq, nbp=nbp, n_blk=n_blk, sel_kc=sel_kc,
                    topk_fn=functools.partial(_topk_roll, n_top=min(N_SEL, n_blk)))


def _nsa_sample(qb, br, nsa_new, win_new, state_win, cache_nsa, page_table, lw, db, tq, n_pp, sel_kc):
    n_pages = page_table.shape[1]
    n_steps = n_pages // n_pp
    past = n_pages * PAGE_SIZE
    l_pad = -(-(past + tq) // SEL_BLOCK) * SEL_BLOCK
    n_blk = l_pad // SEL_BLOCK
    n_cmp = l_pad // CMP_BLOCK
    m_c = -(-n_cmp // SUBLANES) * SUBLANES
    nbp = -(-(n_blk + N_FEAT) // LANES) * LANES
    assert past % sel_kc == 0 and tq <= CMP_BLOCK and l_pad <= past + LANES and n_pp % 2 == 0
    sel_rows = past + LANES
    rr = np.arange(2 * PAGE_SIZE)
    bp2 = 2 * PAGE_SIZE // CMP_BLOCK
    perm_np = np.zeros((2 * PAGE_SIZE, 2 * PAGE_SIZE), np.float32)
    perm_np[rr, (rr % bp2) * CMP_BLOCK + rr // bp2] = 1.0
    perm = jnp.asarray(perm_np, dtype=BF16)
    assert cache_nsa.shape[1:] == (PAGE_SIZE, 4, NSA_KV_HEADS, HEAD_DIM)
    cache_t = _pages_last(cache_nsa)
    wpast = state_win.reshape(db, WINDOW, 2 * KV_W)

    def page_spec(j):
        return pl.BlockSpec((None, 4, KV_W, PAGE_SIZE), lambda bi, s, pt: (pt[bi, s * n_pp + j], 0, 0, 0))
    ws = _cmp_weights(lw)
    pose = _sel_key_features(sel_rows, nbp)
    fqs = jnp.asarray(_query_features(nbp, nbp - N_FEAT))
    featw, fqw = _win_constants()
    consts = ws + [pose, fqs, featw, fqw, perm]

    full = lambda a: pl.BlockSpec(a.shape, lambda bi, s, pt: (0,) * a.ndim,
                                  pipeline_mode=pl.Buffered(1))
    grid_spec = pltpu.PrefetchScalarGridSpec(
        num_scalar_prefetch=1,
        grid=(db, n_steps),
        in_specs=[pl.BlockSpec((tq, NSA_W), lambda bi, s, pt: (bi, 0)),
                  pl.BlockSpec((tq, LANES), lambda bi, s, pt: (bi, 0)),
                  pl.BlockSpec((tq, 4 * KV_W), lambda bi, s, pt: (bi, 0)),
                  pl.BlockSpec((tq, 2 * KV_W), lambda bi, s, pt: (bi, 0)),
                  pl.BlockSpec((None, WINDOW, 2 * KV_W), lambda bi, s, pt: (bi, 0, 0))]
                 + [full(a) for a in consts]
                 + [page_spec(j) for j in range(n_pp)],
        out_specs=pl.BlockSpec((tq, NSA_W), lambda bi, s, pt: (bi, 0)),
        scratch_shapes=[pltpu.VMEM((CMP_BLOCK, m_c, 2 * KV_W), F32),
                        pltpu.VMEM((sel_rows, 2 * KV_W), BF16),
                        pltpu.VMEM((2 * nbp, KV_W), F32),
                        pltpu.VMEM((2 * nbp, KV_W), F32),
                        pltpu.VMEM((N_SLOTS * LANES, 2 * KV_W), F32)],
    )
    return pl.pallas_call(
        functools.partial(_nsa_sample_kernel, n_pp=n_pp, n_steps=n_steps, tq=tq, past=past,
                          m_c=m_c, nbp=nbp, n_blk=n_blk, sel_kc=sel_kc),
        grid_spec=grid_spec,
        out_shape=jax.ShapeDtypeStruct((db * tq, NSA_W), F32),
        compiler_params=_cparams(2),
        name="nsa_sample",
    )(page_table, qb, br, nsa_new, win_new, wpast, *consts, *([cache_t] * n_pp))


def _block_diag2(a):
    z = jnp.zeros_like(a)
    return jnp.concatenate([jnp.concatenate([a, z], axis=-1), jnp.concatenate([z, a], axis=-1)], axis=-2)


def _layer_weights(norm_g, w_in, q_g, kc_g, ks_g, kw_g, pe_k, w1_k, w2_k, pe_v, w1_v, w2_v,
                   w_out, w_ple, w_ple_gate):
    two = lambda g: jnp.concatenate([g, g]).reshape(1, 2 * HEAD_DIM)
    w_br = jnp.pad(w_in[:, C_BR:C_GB], ((0, 0), (0, LANES - 3 * NSA_HEADS)))
    return {
        "norm_g": norm_g.reshape(1, -1),
        "w_main": w_in[:, :C_BR].astype(BF16),
        "w_br": w_br.astype(BF16),
        "w_gb": w_in[:, C_GB:].astype(BF16),
        "q_g2": two(q_g), "ks_g2": two(ks_g), "kw_g2": two(kw_g), "kc_g2": two(kc_g),
        "pe_k2": jnp.concatenate([pe_k, pe_k], axis=-1),
        "pe_v2": jnp.concatenate([pe_v, pe_v], axis=-1),
        "w1k_bd": _block_diag2(w1_k.reshape(CMP_BLOCK, HEAD_DIM, CMP_HIDDEN)).astype(BF16).reshape(
            CMP_BLOCK // 2, 2 * KV_W, NSA_KV_HEADS * CMP_HIDDEN),
        "w1v_bd": _block_diag2(w1_v.reshape(CMP_BLOCK, HEAD_DIM, CMP_HIDDEN)).astype(BF16).reshape(
            CMP_BLOCK // 2, 2 * KV_W, NSA_KV_HEADS * CMP_HIDDEN),
        "w2k_bd": _block_diag2(w2_k).astype(BF16),
        "w2v_bd": _block_diag2(w2_v).astype(BF16),
        "w_out": w_out.astype(BF16),
        "w_pg": w_ple_gate.astype(BF16),
        "w_ple": w_ple.astype(BF16),
    }


def _prompt_layer(x, p, lw):
    b, t, d = x.shape
    x2d = x.reshape(b * t, d)
    qa, sb_rows, gates, qb, nsa_rows, win_rows, br, sb_t, nsa_t, win_t = _project(x2d, lw, tm=512, seq=t)
    o_sb = _sb_prompt(qa, sb_rows, b, t, tq=256)
    kc, vc = _cmp_prompt(nsa_rows, lw, b, t, n_rows=LANES)
    o_nsa = _nsa_prompt(qb, br, kc, vc, nsa_rows, win_rows, b, t, tq=LANES, sel_kc=512)
    y = _output(o_sb, o_nsa, gates, x2d, p.reshape(b * t, -1), lw, tm=512)
    keep = min(WINDOW, t)

    def rows_out(xt, n_a, n_b):
        return jnp.transpose(xt.reshape(b, n_a, n_b, HEAD_DIM, xt.shape[2]), (0, 4, 1, 2, 3))

    return (y.reshape(b, t, d),
            rows_out(sb_t, 2, SB_HEADS),
            rows_out(nsa_t, 4, NSA_KV_HEADS),
            rows_out(win_t[:, :, t - keep:], 2, NSA_KV_HEADS))


def _sample_layer(x, p, cache_sb, cache_nsa, state_win, page_table, lw):
    db, tq, d = x.shape
    n = db * tq
    x2d = x.reshape(n, d)
    qa, sb_rows, gates, qb, nsa_rows, win_rows, br = _project(x2d, lw, tm=n)
    n_pages = page_table.shape[1]
    n_pp = next(c for c in (32, 16, 8) if n_pages % c == 0)
    o_sb = _sb_sample(qa, sb_rows, cache_sb, page_table, db, tq, n_pp=n_pp)
    past = page_table.shape[1] * PAGE_SIZE
    sel_kc = next(c for c in (2048, 1024, 512, 256, 128) if past % c == 0)
    o_nsa = _nsa_sample(qb, br, nsa_rows, win_rows, state_win, cache_nsa, page_table, lw, db, tq,
                        n_pp=n_pp, sel_kc=sel_kc)
    y = _output(o_sb, o_nsa, gates, x2d, p.reshape(n, -1), lw, tm=n)
    win_new = win_rows.reshape(db, tq, 2, NSA_KV_HEADS, HEAD_DIM)
    w_buf = state_win.shape[1]
    win_out = jnp.concatenate([state_win, win_new], axis=1)[:, tq:]
    assert win_out.shape[1] == w_buf
    return (y.reshape(db, tq, d),
            sb_rows.reshape(db, tq, 2, SB_HEADS, HEAD_DIM),
            nsa_rows.reshape(db, tq, 4, NSA_KV_HEADS, HEAD_DIM),
            win_out)


def kernel(x_prompt, x_sample, p_prompt, p_sample, cache_sb, cache_nsa, state_win, page_table, norm_g, w_in, q_norm_g, kcmp_norm_g, ksel_norm_g, kwin_norm_g, cmp_pos_k, cmp_w1_k, cmp_w2_k, cmp_pos_v, cmp_w1_v, cmp_w2_v, w_out, w_ple, w_ple_gate):
    depth = norm_g.shape[0]
    assert state_win.shape[2] == WINDOW and x_prompt.shape[1] >= WINDOW + LANES
    y_p, y_s = x_prompt, x_sample
    outs = [[] for _ in range(6)]
    for i in range(depth):
        lw = _layer_weights(norm_g[i], w_in[i], q_norm_g[i], kcmp_norm_g[i], ksel_norm_g[i],
                            kwin_norm_g[i], cmp_pos_k[i], cmp_w1_k[i], cmp_w2_k[i],
                            cmp_pos_v[i], cmp_w1_v[i], cmp_w2_v[i], w_out[i], w_ple[i], w_ple_gate[i])
        y_p, a, b, c = _prompt_layer(y_p, p_prompt[i], lw)
        outs[0].append(a); outs[2].append(b); outs[4].append(c)
        y_s, a, b, c = _sample_layer(y_s, p_sample[i], cache_sb[i], cache_nsa[i], state_win[i],
                                     page_table, lw)
        outs[1].append(a); outs[3].append(b); outs[5].append(c)
    return (y_p, y_s) + tuple(jnp.stack(o) for o in outs)
```
